```python
import math
import jax, jax.numpy as jnp
from jax import lax
import numpy as np

D_MODEL = 4096
BATCH = 4
SEQ = 2048
DEPTH = 1
DEC_BATCH = 4
DEC_SEQ = 4096
PAST_LEN = 128

HEAD_DIM = 128
N_HEADS_A = 8
A_QK_WIDTH = N_HEADS_A * 2 * HEAD_DIM
A_V_WIDTH = N_HEADS_A * 2 * HEAD_DIM
N_HEADS_B = 16
N_KV_B = 4
B_Q_WIDTH = N_HEADS_B * HEAD_DIM
B_KV_WIDTH = N_KV_B * HEAD_DIM
IN_WIDTH = 2 * A_QK_WIDTH + A_V_WIDTH + B_Q_WIDTH + 2 * B_KV_WIDTH
MIX_WIDTH = A_V_WIDTH + B_Q_WIDTH
N_MEM = 256
N_HEADS_X = 4
X_WIDTH = N_HEADS_X * HEAD_DIM
N_EXPERTS = 16
EC_FACTOR = 2
D_FF = 4096
GRID_W = 64
Q_BLOCK = 128
REL_BUCKETS = 32
REL_MAX_DIST = 128
ROPE_THETA = 10000.0
EPS = 1e-6

kernel_name = "hymba_diffattn_axialgqa_ec_encoder"


def rms_norm(x, g):
    xf = x.astype(jnp.float32)
    y = xf * lax.rsqrt(jnp.mean(xf * xf, axis=-1, keepdims=True) + EPS)
    return (y * g.astype(jnp.float32)).astype(x.dtype)


def to_blocks(x):
    b, s = x.shape[:2]
    return x.reshape((b, s // Q_BLOCK, Q_BLOCK) + x.shape[2:]).swapaxes(0, 1)


def from_blocks(y):
    nb, b, qb = y.shape[:3]
    return y.swapaxes(0, 1).reshape((b, nb * qb) + y.shape[3:])


def rel_bucket(rel):
    nb = REL_BUCKETS // 2
    max_exact = nb // 2
    ret = jnp.where(rel > 0, nb, 0)
    n = jnp.abs(rel)
    nf = jnp.maximum(n, 1).astype(jnp.float32)
    large = max_exact + (jnp.log(nf / max_exact) / math.log(REL_MAX_DIST / max_exact)
                         * (nb - max_exact)).astype(jnp.int32)
    large = jnp.minimum(large, nb - 1)
    return ret + jnp.where(n < max_exact, n, large)


def axial_rope_tables(s):
    rows = s // GRID_W
    row = jnp.repeat(jnp.arange(rows), GRID_W).astype(jnp.float32)
    col = jnp.tile(jnp.arange(GRID_W), rows).astype(jnp.float32)
    half = HEAD_DIM // 2
    inv = 1.0 / (ROPE_THETA ** (jnp.arange(0, half, 2, dtype=jnp.float32) / half))
    ang_r = row[:, None] * inv
    ang_c = col[:, None] * inv
    ang = jnp.concatenate([ang_r, ang_r, ang_c, ang_c], axis=-1)
    return jnp.cos(ang), jnp.sin(ang)


def apply_axial_rope(x, cos, sin):
    half = HEAD_DIM // 2
    quarter = half // 2
    xf = x.astype(jnp.float32)

    def rot(u):
        return jnp.concatenate([-u[..., quarter:], u[..., :quarter]], axis=-1)

    xr = jnp.concatenate([rot(xf[..., :half]), rot(xf[..., half:])], axis=-1)
    return (xf * cos[:, None, :] + xr * sin[:, None, :]).astype(x.dtype)


def diff_attention(q1, q2, k1, k2, v, lam, rel_table):
    s = k1.shape[1]
    scale = HEAD_DIM ** -0.5
    kpos = jnp.arange(s, dtype=jnp.int32)
    qpos = kpos.reshape(s // Q_BLOCK, Q_BLOCK)

    def block(args):
        q1b, q2b, qp = args
        bias = jnp.moveaxis(rel_table[rel_bucket(kpos[None, :] - qp[:, None])], -1, 0)
        bias = bias.astype(jnp.float32)
        s1 = jnp.einsum('bqhd,bkhd->bhqk', q1b, k1).astype(jnp.float32) * scale + bias
        s2 = jnp.einsum('bqhd,bkhd->bhqk', q2b, k2).astype(jnp.float32) * scale + bias
        p = jax.nn.softmax(s1, axis=-1) - lam * jax.nn.softmax(s2, axis=-1)
        return jnp.einsum('bhqk,bkhe->bqhe', p.astype(v.dtype), v)

    return from_blocks(lax.map(block, (to_blocks(q1), to_blocks(q2), qpos)))


def gqa_attention(q, k, v):
    b, s, h, d = q.shape
    g = k.shape[2]
    r = h // g
    scale = d ** -0.5

    def block(qb):
        qg = qb.reshape(b, Q_BLOCK, g, r, d)
        sc = jnp.einsum('bqgrd,bkgd->bgrqk', qg, k).astype(jnp.float32) * scale
        p = jax.nn.softmax(sc, axis=-1)
        o = jnp.einsum('bgrqk,bkgd->bqgrd', p.astype(v.dtype), v)
        return o.reshape(b, Q_BLOCK, h, d)

    return from_blocks(lax.map(block, to_blocks(q)))


def expert_choice_ffn(f, w_router, w_gate, w_up, w_down):
    n = f.shape[0]
    cap = EC_FACTOR * n // N_EXPERTS
    aff = jax.nn.softmax((f @ w_router).astype(jnp.float32), axis=-1)
    gates, idx = lax.top_k(aff.T, cap)
    xe = f[idx]
    hid = jax.nn.silu(jnp.einsum('ecd,edf->ecf', xe, w_gate)) * jnp.einsum('ecd,edf->ecf', xe, w_up)
    ye = jnp.einsum('ecf,efd->ecd', hid, w_down) * gates[..., None].astype(f.dtype)
    return jnp.zeros_like(f).at[idx.reshape(-1)].add(ye.reshape(-1, f.shape[1]))


def trunk(x, mem, rel_table, g_mix, w_in, g_qa, g_ka, lam_q1, lam_k1, lam_q2, lam_k2,
          g_subln, g_qb, g_kb, w_o, g_cross, g_mem, wq_x, wk_x, wv_x, g_qx, g_kx, wo_x,
          g_ffn, w_router, w_gate, w_up, w_down):
    b, s, dm = x.shape
    cos, sin = axial_rope_tables(s)
    splits = np.cumsum([A_QK_WIDTH, A_QK_WIDTH, A_V_WIDTH, B_Q_WIDTH, B_KV_WIDTH]).tolist()
    for l in range(DEPTH):
        h = rms_norm(x, g_mix[l])
        proj = h @ w_in[l]
        qa, ka, va, qb, kb, vb = jnp.split(proj, splits, axis=-1)
        qa = rms_norm(qa.reshape(b, s, N_HEADS_A, 2, HEAD_DIM), g_qa[l])
        ka = rms_norm(ka.reshape(b, s, N_HEADS_A, 2, HEAD_DIM), g_ka[l])
        va = va.reshape(b, s, N_HEADS_A, 2 * HEAD_DIM)
        lam_init = 0.8 - 0.6 * math.exp(-0.3 * l)
        lam = (jnp.exp(jnp.sum(lam_q1[l].astype(jnp.float32) * lam_k1[l].astype(jnp.float32)))
               - jnp.exp(jnp.sum(lam_q2[l].astype(jnp.float32) * lam_k2[l].astype(jnp.float32)))
               + lam_init)
        oa = diff_attention(qa[:, :, :, 0], qa[:, :, :, 1], ka[:, :, :, 0], ka[:, :, :, 1], va, lam, rel_table)
        oa = (rms_norm(oa, g_subln[l]) * (1.0 - lam_init)).reshape(b, s, A_V_WIDTH)
        qb = apply_axial_rope(rms_norm(qb.reshape(b, s, N_HEADS_B, HEAD_DIM), g_qb[l]), cos, sin)
        kb = apply_axial_rope(rms_norm(kb.reshape(b, s, N_KV_B, HEAD_DIM), g_kb[l]), cos, sin)
        vb = vb.reshape(b, s, N_KV_B, HEAD_DIM)
        ob = gqa_attention(qb, kb, vb).reshape(b, s, B_Q_WIDTH)
        x = x + jnp.concatenate([oa, ob], axis=-1) @ w_o[l]
        c = rms_norm(x, g_cross[l])
        m = rms_norm(mem, g_mem[l])
        qx = rms_norm((c @ wq_x[l]).reshape(b, s, N_HEADS_X, HEAD_DIM), g_qx[l])
        kx = rms_norm((m @ wk_x[l]).reshape(b, N_MEM, N_HEADS_X, HEAD_DIM), g_kx[l])
        vx = (m @ wv_x[l]).reshape(b, N_MEM, N_HEADS_X, HEAD_DIM)
        sx = jnp.einsum('bqhd,bkhd->bhqk', qx, kx).astype(jnp.float32) * (HEAD_DIM ** -0.5)
        px = jax.nn.softmax(sx, axis=-1).astype(vx.dtype)
        ox = jnp.einsum('bhqk,bkhd->bqhd', px, vx).reshape(b, s, X_WIDTH)
        x = x + ox @ wo_x[l]
        f = rms_norm(x, g_ffn[l]).reshape(b * s, dm)
        x = x + expert_choice_ffn(f, w_router[l], w_gate[l], w_up[l], w_down[l]).reshape(b, s, dm)
    return x


def setup_inputs(seed: int = 0) -> dict:
    key = jax.random.key(seed)
    ks = iter(jax.random.split(key, 40))
    f32 = jnp.float32

    def nrm(shape, scale):
        return jax.random.normal(next(ks), shape, f32) * scale

    def gain(shape):
        return 1.0 + 0.05 * jax.random.normal(next(ks), shape, f32)

    L, D = DEPTH, D_MODEL
    return {
        "x_prompt": nrm((BATCH, SEQ, D), 1.0),
        "x_sample": nrm((DEC_BATCH, DEC_SEQ, D), 1.0),
        "mem_prompt": nrm((BATCH, N_MEM, D), 1.0),
        "mem_sample": nrm((DEC_BATCH, N_MEM, D), 1.0),
        "rel_table": nrm((REL_BUCKETS, N_HEADS_A), 0.2),
        "g_mix": gain((L, D)),
        "w_in": nrm((L, D, IN_WIDTH), D ** -0.5),
        "g_qa": gain((L, HEAD_DIM)),
        "g_ka": gain((L, HEAD_DIM)),
        "lam_q1": nrm((L, HEAD_DIM), 0.1),
        "lam_k1": nrm((L, HEAD_DIM), 0.1),
        "lam_q2": nrm((L, HEAD_DIM), 0.1),
        "lam_k2": nrm((L, HEAD_DIM), 0.1),
        "g_subln": gain((L, 2 * HEAD_DIM)),
        "g_qb": gain((L, HEAD_DIM)),
        "g_kb": gain((L, HEAD_DIM)),
        "w_o": nrm((L, MIX_WIDTH, D), MIX_WIDTH ** -0.5),
        "g_cross": gain((L, D)),
        "g_mem": gain((L, D)),
        "wq_x": nrm((L, D, X_WIDTH), D ** -0.5),
        "wk_x": nrm((L, D, X_WIDTH), D ** -0.5),
        "wv_x": nrm((L, D, X_WIDTH), D ** -0.5),
        "g_qx": gain((L, HEAD_DIM)),
        "g_kx": gain((L, HEAD_DIM)),
        "wo_x": nrm((L, X_WIDTH, D), X_WIDTH ** -0.5),
        "g_ffn": gain((L, D)),
        "w_router": nrm((L, D, N_EXPERTS), D ** -0.5),
        "w_gate": nrm((L, N_EXPERTS, D, D_FF), D ** -0.5),
        "w_up": nrm((L, N_EXPERTS, D, D_FF), D ** -0.5),
        "w_down": nrm((L, N_EXPERTS, D_FF, D), D_FF ** -0.5),
    }


def reference(x_prompt, x_sample, mem_prompt, mem_sample, rel_table, g_mix, w_in, g_qa, g_ka,
              lam_q1, lam_k1, lam_q2, lam_k2, g_subln, g_qb, g_kb, w_o, g_cross, g_mem,
              wq_x, wk_x, wv_x, g_qx, g_kx, wo_x, g_ffn, w_router, w_gate, w_up, w_down):
    y_prompt = trunk(x_prompt, mem_prompt, rel_table, g_mix, w_in, g_qa, g_ka, lam_q1, lam_k1,
                     lam_q2, lam_k2, g_subln, g_qb, g_kb, w_o, g_cross, g_mem, wq_x, wk_x, wv_x,
                     g_qx, g_kx, wo_x, g_ffn, w_router, w_gate, w_up, w_down)
    y_sample = trunk(x_sample, mem_sample, rel_table, g_mix, w_in, g_qa, g_ka, lam_q1, lam_k1,
                     lam_q2, lam_k2, g_subln, g_qb, g_kb, w_o, g_cross, g_mem, wq_x, wk_x, wv_x,
                     g_qx, g_kx, wo_x, g_ffn, w_router, w_gate, w_up, w_down)
    return (y_prompt, y_sample)
```

```python
import functools
import math

import numpy as np
import jax
import jax.numpy as jnp
from jax import lax
from jax.experimental import pallas as pl
from jax.experimental.pallas import tpu as pltpu

F32 = jnp.float32
BF16 = jnp.bfloat16

HEAD_DIM = 128
N_HEADS_A = 8
N_HEADS_B = 16
N_KV_B = 4
N_HEADS_X = 4
N_EXPERTS = 16
EC_FACTOR = 2
GRID_W = 64
REL_BUCKETS = 32
REL_MAX_DIST = 128
ROPE_THETA = 10000.0
EPS = 1e-6
ATTN_SCALE = HEAD_DIM ** -0.5

V7X_VMEM_BYTES = 64 * 1024 * 1024
VMEM_LIMIT = V7X_VMEM_BYTES - 8 * 1024 * 1024
LANES = 128
SUBLANES = 8

NT_DIMS = (((1,), (1,)), ((), ()))


def _cparams(*sem):
    return pltpu.CompilerParams(dimension_semantics=sem, vmem_limit_bytes=VMEM_LIMIT)


def _rms(x, g):
    return x * lax.rsqrt(jnp.mean(x * x, axis=-1, keepdims=True) + EPS) * g


def _rmsnorm_body(x_ref, g_ref, o_ref):
    o_ref[...] = _rms(x_ref[...], g_ref[...]).astype(o_ref.dtype)


def rmsnorm_cast(x, g, tm=256):
    m, d = x.shape
    tm = min(tm, m)
    return pl.pallas_call(
        _rmsnorm_body,
        grid=(m // tm,),
        in_specs=[pl.BlockSpec((tm, d), lambda i: (i, 0)), pl.BlockSpec((1, d), lambda i: (0, 0))],
        out_specs=pl.BlockSpec((tm, d), lambda i: (i, 0)),
        out_shape=jax.ShapeDtypeStruct((m, d), BF16),
        compiler_params=_cparams("arbitrary"),
        name="rmsnorm_cast",
    )(x, g.reshape(1, d))


def _mm_body(*refs, n_a, has_res):
    a_refs, w_ref = refs[:n_a], refs[n_a]
    r_ref = refs[n_a + 1] if has_res else None
    o_ref, wb_ref = refs[-2:]

    @pl.when(pl.program_id(1) == 0)
    def _():
        wb_ref[...] = w_ref[...].astype(BF16)

    acc, k0 = None, 0
    for a_ref in a_refs:
        k = a_ref.shape[1]
        part = jnp.dot(a_ref[...], wb_ref[k0:k0 + k, :], preferred_element_type=F32)
        acc = part if acc is None else acc + part
        k0 += k
    if has_res:
        acc = acc + r_ref[...]
    o_ref[...] = acc


def matmul(a_list, w, res=None, tm=512, tn=512):
    m = a_list[0].shape[0]
    k, n = w.shape
    assert sum(a.shape[1] for a in a_list) == k
    tm, tn = min(tm, m), min(tn, n)
    in_specs = [pl.BlockSpec((tm, a.shape[1]), lambda j, i: (i, 0)) for a in a_list]
    in_specs.append(pl.BlockSpec((k, tn), lambda j, i: (0, j)))
    args = list(a_list) + [w]
    if res is not None:
        in_specs.append(pl.BlockSpec((tm, tn), lambda j, i: (i, j)))
        args.append(res)
    return pl.pallas_call(
        functools.partial(_mm_body, n_a=len(a_list), has_res=res is not None),
        grid=(n // tn, m // tm),
        in_specs=in_specs,
        out_specs=pl.BlockSpec((tm, tn), lambda j, i: (i, j)),
        out_shape=jax.ShapeDtypeStruct((m, n), F32),
        scratch_shapes=[pltpu.VMEM((k, tn), BF16)],
        compiler_params=_cparams("arbitrary", "arbitrary"),
        name="matmul",
    )(*args)


def _online_softmax_step(s, v, m_ref, l_ref, acc_ref):
    m_old = m_ref[...]
    m_new = jnp.maximum(m_old, jnp.max(s, axis=-1, keepdims=True))
    alpha = jnp.exp(m_old - m_new)
    p = jnp.exp(s - m_new)
    l_ref[...] = alpha * l_ref[...] + jnp.sum(p, axis=-1, keepdims=True)
    acc_ref[...] = alpha * acc_ref[...] + jnp.dot(p.astype(BF16), v, preferred_element_type=F32)
    m_ref[...] = m_new


def _rope(x, cos, sin_signed):
    lane = lax.broadcasted_iota(jnp.int32, x.shape, 1)
    first = (lane % (HEAD_DIM // 2)) < (HEAD_DIM // 4)
    xr = jnp.where(first, pltpu.roll(x, HEAD_DIM - HEAD_DIM // 4, 1), pltpu.roll(x, HEAD_DIM // 4, 1))
    return x * cos + xr * sin_signed


def _diff_attn_body(q_ref, k_ref, v_ref, bias_ref, gq_ref, gk_ref, gs_ref, lam_ref, o_ref,
                    kn_ref, vb_ref, m_ref, l_ref, acc_ref, *, t, n_kt, lam_init):
    qi = pl.program_id(2)
    d = HEAD_DIM

    @pl.when(qi == 0)
    def _():
        for j in range(2):
            kj = k_ref[0, :, j * d:(j + 1) * d]
            kn_ref[:, j * d:(j + 1) * d] = _rms(kj, gk_ref[...]).astype(BF16)
        vb_ref[...] = v_ref[0].astype(BF16)

    qs = []
    for j in range(2):
        qj = q_ref[0, :, j * d:(j + 1) * d]
        qs.append((_rms(qj, gq_ref[...]) * ATTN_SCALE).astype(BF16))

    m_ref[...] = jnp.full(m_ref.shape, -jnp.inf, F32)
    l_ref[...] = jnp.zeros(l_ref.shape, F32)
    acc_ref[...] = jnp.zeros(acc_ref.shape, F32)

    def step(kt, carry):
        k0 = pl.multiple_of(kt * t, t)
        bias = bias_ref[0, jnp.clip(kt - qi, -2, 2) + 2]
        vt = vb_ref[pl.ds(k0, t), :]
        for j in range(2):
            kjt = kn_ref[pl.ds(k0, t), j * d:(j + 1) * d]
            s = lax.dot_general(qs[j], kjt, NT_DIMS, preferred_element_type=F32) + bias
            _online_softmax_step(s, vt, m_ref.at[j], l_ref.at[j], acc_ref.at[j])
        return carry

    lax.fori_loop(0, n_kt, step, 0)

    lam_v = lam_ref[...]
    lam = (jnp.exp(jnp.sum(lam_v[0:1] * lam_v[1:2], axis=-1, keepdims=True))
           - jnp.exp(jnp.sum(lam_v[2:3] * lam_v[3:4], axis=-1, keepdims=True)) + lam_init)
    o = acc_ref[0] / l_ref[0] - lam * (acc_ref[1] / l_ref[1])
    o_ref[0] = (_rms(o, gs_ref[...]) * (1.0 - lam_init)).astype(o_ref.dtype)


def diff_attention(proj, bias_tiles, g_qa, g_ka, g_subln, lam_vecs, lam_init, t):
    b, s, _ = proj.shape
    h = N_HEADS_A
    w = 2 * HEAD_DIM
    n_kt = s // t
    return pl.pallas_call(
        functools.partial(_diff_attn_body, t=t, n_kt=n_kt, lam_init=lam_init),
        grid=(b, h, n_kt),
        in_specs=[
            pl.BlockSpec((1, t, w), lambda bi, hi, qi: (bi, qi, hi)),
            pl.BlockSpec((1, s, w), lambda bi, hi, qi: (bi, 0, h + hi)),
            pl.BlockSpec((1, s, w), lambda bi, hi, qi: (bi, 0, 2 * h + hi)),
            pl.BlockSpec((1, 5, t, t), lambda bi, hi, qi: (hi, 0, 0, 0)),
            pl.BlockSpec((1, HEAD_DIM), lambda bi, hi, qi: (0, 0)),
            pl.BlockSpec((1, HEAD_DIM), lambda bi, hi, qi: (0, 0)),
            pl.BlockSpec((1, w), lambda bi, hi, qi: (0, 0)),
            pl.BlockSpec((4, HEAD_DIM), lambda bi, hi, qi: (0, 0)),
        ],
        out_specs=pl.BlockSpec((1, t, w), lambda bi, hi, qi: (bi, qi, hi)),
        out_shape=jax.ShapeDtypeStruct((b, s, h * w), BF16),
        scratch_shapes=[
            pltpu.VMEM((s, w), BF16), pltpu.VMEM((s, w), BF16),
            pltpu.VMEM((2, t, 1), F32), pltpu.VMEM((2, t, 1), F32), pltpu.VMEM((2, t, w), F32),
        ],
        compiler_params=_cparams("arbitrary", "arbitrary", "arbitrary"),
        name="diff_attention",
    )(proj, proj, proj, bias_tiles, g_qa, g_ka, g_subln, lam_vecs)


def _gqa_body(q_ref, k_ref, v_ref, cq_ref, sq_ref, ck_ref, sk_ref, gq_ref, gk_ref, o_ref,
              kn_ref, vb_ref, m_ref, l_ref, acc_ref, *, tq, tk, n_kt, rep):
    qi = pl.program_id(2)
    d = HEAD_DIM

    @pl.when(qi == 0)
    def _():
        kn = _rope(_rms(k_ref[0], gk_ref[...]), ck_ref[...], sk_ref[...])
        kn_ref[...] = kn.astype(BF16)
        vb_ref[...] = v_ref[0].astype(BF16)

    cq, sq = cq_ref[...], sq_ref[...]
    qs = []
    for r in range(rep):
        qr = _rope(_rms(q_ref[0, :, r * d:(r + 1) * d], gq_ref[...]), cq, sq)
        qs.append((qr * ATTN_SCALE).astype(BF16))
    q_all = jnp.concatenate(qs, axis=0)

    m_ref[...] = jnp.full(m_ref.shape, -jnp.inf, F32)
    l_ref[...] = jnp.zeros(l_ref.shape, F32)
    acc_ref[...] = jnp.zeros(acc_ref.shape, F32)

    def step(kt, carry):
        k0 = pl.multiple_of(kt * tk, tk)
        s = lax.dot_general(q_all, kn_ref[pl.ds(k0, tk), :], NT_DIMS, preferred_element_type=F32)
        _online_softmax_step(s, vb_ref[pl.ds(k0, tk), :], m_ref, l_ref, acc_ref)
        return carry

    lax.fori_loop(0, n_kt, step, 0)
    o = acc_ref[...] / l_ref[...]
    for r in range(rep):
        o_ref[0, :, r * d:(r + 1) * d] = o[r * tq:(r + 1) * tq].astype(o_ref.dtype)


def gqa_attention(proj, cos, sin_signed, g_qb, g_kb, tq=128, tk=512):
    b, s, _ = proj.shape
    g = N_KV_B
    rep = N_HEADS_B // N_KV_B
    d = HEAD_DIM
    qw = rep * d
    tq, tk = min(tq, s), min(tk, s)
    a_w = N_HEADS_A * 2 * d
    q_blk0 = 3 * a_w // qw
    k_blk0 = (3 * a_w + N_HEADS_B * d) // d
    v_blk0 = k_blk0 + g
    return pl.pallas_call(
        functools.partial(_gqa_body, tq=tq, tk=tk, n_kt=s // tk, rep=rep),
        grid=(b, g, s // tq),
        in_specs=[
            pl.BlockSpec((1, tq, qw), lambda bi, gi, qi: (bi, qi, q_blk0 + gi)),
            pl.BlockSpec((1, s, d), lambda bi, gi, qi: (bi, 0, k_blk0 + gi)),
            pl.BlockSpec((1, s, d), lambda bi, gi, qi: (bi, 0, v_blk0 + gi)),
            pl.BlockSpec((tq, d), lambda bi, gi, qi: (qi, 0)),
            pl.BlockSpec((tq, d), lambda bi, gi, qi: (qi, 0)),
            pl.BlockSpec((s, d), lambda bi, gi, qi: (0, 0)),
            pl.BlockSpec((s, d), lambda bi, gi, qi: (0, 0)),
            pl.BlockSpec((1, d), lambda bi, gi, qi: (0, 0)),
            pl.BlockSpec((1, d), lambda bi, gi, qi: (0, 0)),
        ],
        out_specs=pl.BlockSpec((1, tq, qw), lambda bi, gi, qi: (bi, qi, gi)),
        out_shape=jax.ShapeDtypeStruct((b, s, g * qw), BF16),
        scratch_shapes=[
            pltpu.VMEM((s, d), BF16), pltpu.VMEM((s, d), BF16),
            pltpu.VMEM((rep * tq, 1), F32), pltpu.VMEM((rep * tq, 1), F32), pltpu.VMEM((rep * tq, d), F32),
        ],
        compiler_params=_cparams("arbitrary", "arbitrary", "arbitrary"),
        name="gqa_attention",
    )(proj, proj, proj, cos, sin_signed, cos, sin_signed, g_qb, g_kb)


def _cross_attn_body(q_ref, k_ref, v_ref, gq_ref, gk_ref, o_ref):
    d = HEAD_DIM
    for hh in range(N_HEADS_X):
        sl = slice(hh * d, (hh + 1) * d)
        qn = (_rms(q_ref[0, :, sl], gq_ref[...]) * ATTN_SCALE).astype(BF16)
        kn = _rms(k_ref[0, :, sl], gk_ref[...]).astype(BF16)
        s = lax.dot_general(qn, kn, NT_DIMS, preferred_element_type=F32)
        p = jnp.exp(s - jnp.max(s, axis=-1, keepdims=True))
        o = jnp.dot(p.astype(BF16), v_ref[0, :, sl].astype(BF16), preferred_element_type=F32)
        o_ref[0, :, sl] = (o / jnp.sum(p, axis=-1, keepdims=True)).astype(o_ref.dtype)


def cross_attention(qx, kx, vx, g_qx, g_kx, tq=512):
    b, s, w = qx.shape
    n_mem = kx.shape[1]
    tq = min(tq, s)
    return pl.pallas_call(
        _cross_attn_body,
        grid=(b, s // tq),
        in_specs=[
            pl.BlockSpec((1, tq, w), lambda bi, qi: (bi, qi, 0)),
            pl.BlockSpec((1, n_mem, w), lambda bi, qi: (bi, 0, 0)),
            pl.BlockSpec((1, n_mem, w), lambda bi, qi: (bi, 0, 0)),
            pl.BlockSpec((1, HEAD_DIM), lambda bi, qi: (0, 0)),
            pl.BlockSpec((1, HEAD_DIM), lambda bi, qi: (0, 0)),
        ],
        out_specs=pl.BlockSpec((1, tq, w), lambda bi, qi: (bi, qi, 0)),
        out_shape=jax.ShapeDtypeStruct((b, s, w), BF16),
        compiler_params=_cparams("arbitrary", "arbitrary"),
        name="cross_attention",
    )(qx, kx, vx, g_qx, g_kx)


def _split_bf16(x):
    hi = x.astype(BF16)
    lo = (x - hi.astype(F32)).astype(BF16)
    return hi, lo


def _ffn_norm_body(x_ref, g_ref, wr_ref, fp_ref, lg_ref):
    f = _rms(x_ref[...], g_ref[...])
    half = f.shape[1] // 2
    fr = f.astype(BF16).astype(F32)
    lo_bits = pltpu.bitcast(fr[:, :half], jnp.uint32) >> 16
    hi_bits = pltpu.bitcast(fr[:, half:], jnp.uint32) & jnp.uint32(0xFFFF0000)
    fp_ref[...] = lo_bits | hi_bits
    f_hi, f_lo = _split_bf16(f)
    w_hi, w_lo = _split_bf16(wr_ref[...])
    lg_ref[...] = (jnp.dot(f_hi, w_hi, preferred_element_type=F32)
                   + jnp.dot(f_hi, w_lo, preferred_element_type=F32)
                   + jnp.dot(f_lo, w_hi, preferred_element_type=F32))


def ffn_norm_router(x, g, w_router, tm=256):
    m, d = x.shape
    e = w_router.shape[1]
    tm = min(tm, m)
    return pl.pallas_call(
        _ffn_norm_body,
        grid=(m // tm,),
        in_specs=[
            pl.BlockSpec((tm, d), lambda i: (i, 0)),
            pl.BlockSpec((1, d), lambda i: (0, 0)),
            pl.BlockSpec((d, e), lambda i: (0, 0)),
        ],
        out_specs=[pl.BlockSpec((tm, d // 2), lambda i: (i, 0)), pl.BlockSpec((tm, e), lambda i: (i, 0))],
        out_shape=[jax.ShapeDtypeStruct((m, d // 2), jnp.uint32), jax.ShapeDtypeStruct((m, e), F32)],
        compiler_params=_cparams("arbitrary"),
        name="ffn_norm_router",
    )(x, g.reshape(1, d), w_router)


def _unpack_rows(p_ref, o_ref, rows=256):
    m, half = p_ref.shape[1], p_ref.shape[2]
    rows = min(rows, m)
    for r0 in range(0, m, rows):
        p = p_ref[0, r0:r0 + rows, :]
        o_ref[r0:r0 + rows, :half] = pltpu.bitcast(p << 16, F32).astype(BF16)
        o_ref[r0:r0 + rows, half:] = pltpu.bitcast(p & jnp.uint32(0xFFFF0000), F32).astype(BF16)


def _select_top_cap(lg_ref, aff_ref, sel_ref, cs_ref, *, n, cap):
    e = lg_ref.shape[0]
    lg = lg_ref[...]
    ex = jnp.exp(lg - jnp.max(lg, axis=0, keepdims=True))
    aff = ex / jnp.sum(ex, axis=0, keepdims=True)
    aff_ref[...] = aff
    bits = pltpu.bitcast(aff, jnp.int32)

    def search(i, prefix):
        cand = prefix | (jnp.int32(1) << (30 - i))
        cnt = jnp.sum(jnp.where(bits >= cand, 1.0, 0.0), axis=1, keepdims=True)
        return jnp.where(cnt >= cap, cand, prefix)

    thr = lax.fori_loop(0, 31, search, jnp.zeros((e, 1), jnp.int32))
    gt = bits > thr
    need = cap - jnp.sum(jnp.where(gt, 1.0, 0.0), axis=1, keepdims=True)

    tri = (lax.broadcasted_iota(jnp.int32, (LANES, LANES), 0)
           <= lax.broadcasted_iota(jnp.int32, (LANES, LANES), 1)).astype(BF16)

    def cumsum_lanes(ref):
        def blk(j, carry):
            c0 = pl.multiple_of(j * LANES, LANES)
            w = jnp.dot(ref[:, pl.ds(c0, LANES)].astype(BF16), tri, preferred_element_type=F32) + carry
            ref[:, pl.ds(c0, LANES)] = w
            return w[:, LANES - 1:LANES]
        lax.fori_loop(0, n // LANES, blk, jnp.zeros((e, 1), F32))

    eq = bits == thr
    cs_ref[...] = jnp.where(eq, 1.0, 0.0)
    cumsum_lanes(cs_ref)
    sel = gt | (eq & (cs_ref[...] <= need))
    sel_ref[...] = jnp.where(sel, 1.0, 0.0)
    cs_ref[...] = sel_ref[...]
    cumsum_lanes(cs_ref)
    cs_ref[...] = cs_ref[...] * sel_ref[...]


def _route_body(lg_ref, idx_ref, gate_ref, aff_ref, sel_ref, cs_ref, aff_rows, cs_rows, *, n, cap):
    ei = pl.program_id(0)
    n_blk = n // LANES

    @pl.when(ei == 0)
    def _():
        _select_top_cap(lg_ref, aff_ref, sel_ref, cs_ref, n=n, cap=cap)
        for k in range(lg_ref.shape[0]):
            aff_rows[k] = aff_ref[k:k + 1, :]
            cs_rows[k] = cs_ref[k:k + 1, :]

    lane = lax.broadcasted_iota(jnp.int32, (SUBLANES, LANES), 1).astype(F32)
    sub = lax.broadcasted_iota(jnp.int32, (SUBLANES, 1), 0).astype(F32)

    def per_chunk(ci, carry):
        s0 = pl.multiple_of(ci * SUBLANES, SUBLANES)
        want = sub + (s0 + 1).astype(F32)

        def per_blk(j, accs):
            acc_i, acc_g = accs
            c0 = pl.multiple_of(j * LANES, LANES)
            hit = cs_rows[ei, :, pl.ds(c0, LANES)] == want
            acc_i = acc_i + jnp.where(hit, lane + c0.astype(F32), 0.0)
            acc_g = acc_g + jnp.where(hit, aff_rows[ei, :, pl.ds(c0, LANES)], 0.0)
            return acc_i, acc_g

        z = jnp.zeros((SUBLANES, LANES), F32)
        acc_i, acc_g = lax.fori_loop(0, n_blk, per_blk, (z, z), unroll=8 if n_blk % 8 == 0 else 1)
        idx_ref[0, pl.ds(s0, SUBLANES), :] = jnp.sum(acc_i, axis=1, keepdims=True).astype(jnp.int32)
        gate_ref[0, pl.ds(s0, SUBLANES), :] = jnp.sum(acc_g, axis=1, keepdims=True)
        return carry

    lax.fori_loop(0, cap // SUBLANES, per_chunk, 0)


def route(logits_t, cap):
    e, n = logits_t.shape
    return pl.pallas_call(
        functools.partial(_route_body, n=n, cap=cap),
        grid=(e,),
        in_specs=[pl.BlockSpec((e, n), lambda i: (0, 0))],
        out_specs=[pl.BlockSpec((1, cap, 1), lambda i: (i, 0, 0)), pl.BlockSpec((1, cap, 1), lambda i: (i, 0, 0))],
        out_shape=[jax.ShapeDtypeStruct((e, cap, 1), jnp.int32), jax.ShapeDtypeStruct((e, cap, 1), F32)],
        scratch_shapes=[pltpu.VMEM((e, n), F32), pltpu.VMEM((e, n), F32), pltpu.VMEM((e, n), F32),
                        pltpu.VMEM((e, 1, n), F32), pltpu.VMEM((e, 1, n), F32)],
        compiler_params=_cparams("arbitrary"),
        name="route",
    )(logits_t)


def _row_copy(src_ref, dst_ref, src_row, dst_row, sem):
    return pltpu.make_async_copy(src_ref.at[pl.ds(src_row, 1)], dst_ref.at[pl.ds(dst_row, 1)], sem)


def _gather_body(idx_ref, src_ref, o_ref, sem, *, chunk):
    base = pl.program_id(0) * chunk

    def issue(r, c):
        _row_copy(src_ref, o_ref, idx_ref[base + r], base + r, sem).start()
        return c

    def drain(r, c):
        _row_copy(src_ref, o_ref, 0, base + r, sem).wait()
        return c

    lax.fori_loop(0, chunk, issue, 0)
    lax.fori_loop(0, chunk, drain, 0)


def gather_rows(src, idx, chunk=256):
    r = idx.shape[0]
    chunk = min(chunk, r)
    return pl.pallas_call(
        functools.partial(_gather_body, chunk=chunk),
        grid_spec=pltpu.PrefetchScalarGridSpec(
            num_scalar_prefetch=1,
            grid=(r // chunk,),
            in_specs=[pl.BlockSpec(memory_space=pl.ANY)],
            out_specs=pl.BlockSpec(memory_space=pl.ANY),
            scratch_shapes=[pltpu.SemaphoreType.DMA],
        ),
        out_shape=jax.ShapeDtypeStruct((r, src.shape[1]), src.dtype),
        compiler_params=_cparams("arbitrary"),
        name="gather_rows",
    )(idx, src)


def _gate_up_body(xe_ref, wg_ref, wu_ref, o_ref, xb_ref):
    @pl.when(pl.program_id(2) == 0)
    def _():
        _unpack_rows(xe_ref, xb_ref)

    x = xb_ref[...]
    g = jnp.dot(x, wg_ref[0].astype(BF16), preferred_element_type=F32)
    u = jnp.dot(x, wu_ref[0].astype(BF16), preferred_element_type=F32)
    o_ref[0] = (g * (1.0 / (1.0 + jnp.exp(-g))) * u).astype(o_ref.dtype)


def expert_gate_up(xe, w_gate, w_up, tm=1024, tf=256):
    e, c, dh = xe.shape
    d = 2 * dh
    f = w_gate.shape[2]
    tm, tf = min(tm, c), min(tf, f)
    return pl.pallas_call(
        _gate_up_body,
        grid=(e, c // tm, f // tf),
        in_specs=[
            pl.BlockSpec((1, tm, dh), lambda ei, mi, fi: (ei, mi, 0)),
            pl.BlockSpec((1, d, tf), lambda ei, mi, fi: (ei, 0, fi)),
            pl.BlockSpec((1, d, tf), lambda ei, mi, fi: (ei, 0, fi)),
        ],
        out_specs=pl.BlockSpec((1, tm, tf), lambda ei, mi, fi: (ei, mi, fi)),
        out_shape=jax.ShapeDtypeStruct((e, c, f), BF16),
        scratch_shapes=[pltpu.VMEM((tm, d), BF16)],
        compiler_params=_cparams("arbitrary", "arbitrary", "arbitrary"),
        name="expert_gate_up",
    )(xe, w_gate, w_up)


def _down_body(h_ref, w_ref, gate_ref, o_ref, wb_ref):
    @pl.when(pl.program_id(2) == 0)
    def _():
        wb_ref[...] = w_ref[0].astype(BF16)

    o_ref[0] = jnp.dot(h_ref[0], wb_ref[...], preferred_element_type=F32) * gate_ref[0]


def expert_down(hid, w_down, gates, tm=512, tn=512):
    e, c, f = hid.shape
    d = w_down.shape[2]
    tm, tn = min(tm, c), min(tn, d)
    return pl.pallas_call(
        _down_body,
        grid=(e, d // tn, c // tm),
        in_specs=[
            pl.BlockSpec((1, tm, f), lambda ei, ni, mi: (ei, mi, 0)),
            pl.BlockSpec((1, f, tn), lambda ei, ni, mi: (ei, 0, ni)),
            pl.BlockSpec((1, tm, 1), lambda ei, ni, mi: (ei, mi, 0)),
        ],
        out_specs=pl.BlockSpec((1, tm, tn), lambda ei, ni, mi: (ei, mi, ni)),
        out_shape=jax.ShapeDtypeStruct((e, c, d), F32),
        scratch_shapes=[pltpu.VMEM((f, tn), BF16)],
        compiler_params=_cparams("arbitrary", "arbitrary", "arbitrary"),
        name="expert_down",
    )(hid, w_down, gates)


def _combine_body(idx_ref, ye_ref, x_ref, o_ref, buf_ref, sem_in, sem_out, *, rows):
    del x_ref
    base = pl.program_id(0) * rows

    def fetch(r, c):
        _row_copy(o_ref, buf_ref, idx_ref[base + r], r, sem_in).start()
        return c

    def fetch_wait(r, c):
        _row_copy(o_ref, buf_ref, 0, r, sem_in).wait()
        return c

    def put(r, c):
        _row_copy(buf_ref, o_ref, r, idx_ref[base + r], sem_out).start()
        return c

    def put_wait(r, c):
        _row_copy(buf_ref, o_ref, r, 0, sem_out).wait()
        return c

    lax.fori_loop(0, rows, fetch, 0)
    lax.fori_loop(0, rows, fetch_wait, 0)
    buf_ref[...] = buf_ref[...] + ye_ref[...]
    lax.fori_loop(0, rows, put, 0)
    lax.fori_loop(0, rows, put_wait, 0)


def combine(x, ye, idx, rows):
    r, d = ye.shape
    return pl.pallas_call(
        functools.partial(_combine_body, rows=rows),
        grid_spec=pltpu.PrefetchScalarGridSpec(
            num_scalar_prefetch=1,
            grid=(r // rows,),
            in_specs=[pl.BlockSpec((rows, d), lambda i, idx_ref: (i, 0)), pl.BlockSpec(memory_space=pl.ANY)],
            out_specs=pl.BlockSpec(memory_space=pl.ANY),
            scratch_shapes=[pltpu.VMEM((rows, d), F32), pltpu.SemaphoreType.DMA, pltpu.SemaphoreType.DMA],
        ),
        out_shape=jax.ShapeDtypeStruct(x.shape, x.dtype),
        input_output_aliases={2: 0},
        compiler_params=_cparams("arbitrary"),
        name="combine",
    )(idx, ye, x)


def _rel_bucket(rel):
    nb = REL_BUCKETS // 2
    max_exact = nb // 2
    ret = jnp.where(rel > 0, nb, 0)
    n = jnp.abs(rel)
    nf = jnp.maximum(n, 1).astype(F32)
    large = max_exact + (jnp.log(nf / max_exact) / math.log(REL_MAX_DIST / max_exact)
                         * (nb - max_exact)).astype(jnp.int32)
    large = jnp.minimum(large, nb - 1)
    return ret + jnp.where(n < max_exact, n, large)


def _bias_tiles(rel_table, t):
    r = np.arange(t)[:, None]
    c = np.arange(t)[None, :]
    rel = np.stack([dt * t + (c - r) for dt in (-2, -1, 0, 1, 2)]).astype(np.int32)
    return jnp.moveaxis(rel_table[_rel_bucket(jnp.asarray(rel))], -1, 0).astype(F32)


def _axial_rope_tables(s):
    rows = s // GRID_W
    row = jnp.repeat(jnp.arange(rows), GRID_W).astype(F32)
    col = jnp.tile(jnp.arange(GRID_W), rows).astype(F32)
    half = HEAD_DIM // 2
    inv = 1.0 / (ROPE_THETA ** (jnp.arange(0, half, 2, dtype=F32) / half))
    ang_r = row[:, None] * inv
    ang_c = col[:, None] * inv
    ang = jnp.concatenate([ang_r, ang_r, ang_c, ang_c], axis=-1)
    sign = np.where((np.arange(HEAD_DIM) % half) < half // 2, -1.0, 1.0).astype(np.float32)
    return jnp.cos(ang), jnp.sin(ang) * sign


def _layer(x, mem, rel_table, g_mix, w_in, g_qa, g_ka, lam_vecs, g_subln, g_qb, g_kb, w_o, g_cross,
           g_mem, wq_x, wk_x, wv_x, g_qx, g_kx, wo_x, g_ffn, w_router, w_gate, w_up, w_down, lam_init):
    b, s, dm = x.shape
    n = b * s
    n_mem = mem.shape[1]
    x2d = x.reshape(n, dm)
    row = lambda v: v.reshape(1, -1)

    h = rmsnorm_cast(x2d, g_mix)
    proj = matmul([h], w_in).reshape(b, s, -1)
    t_a = min(256, s)
    assert s % t_a == 0 and t_a >= REL_MAX_DIST, "bias tiles need saturated buckets two tiles away"
    oa = diff_attention(proj, _bias_tiles(rel_table, t_a), row(g_qa), row(g_ka), row(g_subln), lam_vecs,
                        lam_init, t_a)
    cos, sin_signed = _axial_rope_tables(s)
    ob = gqa_attention(proj, cos, sin_signed, row(g_qb), row(g_kb))
    x1 = matmul([oa.reshape(n, -1), ob.reshape(n, -1)], w_o, res=x2d)

    c = rmsnorm_cast(x1, g_cross)
    m = rmsnorm_cast(mem.reshape(b * n_mem, dm), g_mem)
    qx = matmul([c], wq_x).reshape(b, s, -1)
    kx = matmul([m], wk_x).reshape(b, n_mem, -1)
    vx = matmul([m], wv_x).reshape(b, n_mem, -1)
    ox = cross_attention(qx, kx, vx, row(g_qx), row(g_kx))
    x2 = matmul([ox.reshape(n, -1)], wo_x, res=x1)

    e = w_router.shape[1]
    cap = EC_FACTOR * n // e
    f_packed, logits = ffn_norm_router(x2, g_ffn, w_router)
    idx, gates = route(logits.T, cap)
    idx_flat = idx.reshape(e * cap)
    xe = gather_rows(f_packed, idx_flat).reshape(e, cap, dm // 2)
    hid = expert_gate_up(xe, w_gate, w_up)
    ye = expert_down(hid, w_down, gates)
    out = combine(x2, ye.reshape(e * cap, dm), idx_flat, rows=min(128, cap))
    return out.reshape(b, s, dm)


def kernel(x_prompt, x_sample, mem_prompt, mem_sample, rel_table, g_mix, w_in, g_qa, g_ka, lam_q1, lam_k1, lam_q2, lam_k2, g_subln, g_qb, g_kb, w_o, g_cross, g_mem, wq_x, wk_x, wv_x, g_qx, g_kx, wo_x, g_ffn, w_router, w_gate, w_up, w_down):
    outs = []
    for x, mem in ((x_prompt, mem_prompt), (x_sample, mem_sample)):
        for l in range(g_mix.shape[0]):
            lam_init = 0.8 - 0.6 * math.exp(-0.3 * l)
            lam_vecs = jnp.stack([lam_q1[l], lam_k1[l], lam_q2[l], lam_k2[l]])
            x = _layer(x, mem, rel_table, g_mix[l], w_in[l], g_qa[l], g_ka[l], lam_vecs, g_subln[l], g_qb[l],
                       g_kb[l], w_o[l], g_cross[l], g_mem[l], wq_x[l], wk_x[l], wv_x[l], g_qx[l], g_kx[l],
                       wo_x[l], g_ffn[l], w_router[l], w_gate[l], w_up[l], w_down[l], lam_init)
        outs.append(x)
    return tuple(outs)
```

```python
import functools
import math

import numpy as np
import jax
import jax.numpy as jnp
from jax import lax
from jax.experimental import pallas as pl
from jax.experimental.pallas import tpu as pltpu

F32 = jnp.float32
BF16 = jnp.bfloat16

HEAD_DIM = 128
N_HEADS_A = 8
N_HEADS_B = 16
N_KV_B = 4
N_HEADS_X = 4
EC_FACTOR = 2
GRID_W = 64
REL_BUCKETS = 32
REL_MAX_DIST = 128
ROPE_THETA = 10000.0
EPS = 1e-6
ATTN_SCALE = HEAD_DIM ** -0.5
LOG2E = math.log2(math.e)

V7X_VMEM_BYTES = 64 * 1024 * 1024
VMEM_LIMIT = V7X_VMEM_BYTES - 8 * 1024 * 1024
LANES = 128
SUBLANES = 8

SAFE_SCORE_BOUND = 30.0
BF16_NORM_SLACK = (1.0 + 2.0 ** -8) ** 2

NT_DIMS = (((1,), (1,)), ((), ()))


def _cparams(*sem):
    return pltpu.CompilerParams(dimension_semantics=sem, vmem_limit_bytes=VMEM_LIMIT)


def _rms(x, g):
    return x * lax.rsqrt(jnp.mean(x * x, axis=-1, keepdims=True) + EPS) * g


def _smem_spec():
    return pl.BlockSpec(memory_space=pltpu.SMEM)


def _rmsnorm_body(x_ref, g_ref, o_ref):
    o_ref[...] = _rms(x_ref[...], g_ref[...]).astype(o_ref.dtype)


def rmsnorm_cast(x, g, tm=256):
    m, d = x.shape
    tm = min(tm, m)
    return pl.pallas_call(
        _rmsnorm_body,
        grid=(m // tm,),
        in_specs=[pl.BlockSpec((tm, d), lambda i: (i, 0)), pl.BlockSpec((1, d), lambda i: (0, 0))],
        out_specs=pl.BlockSpec((tm, d), lambda i: (i, 0)),
        out_shape=jax.ShapeDtypeStruct((m, d), BF16),
        compiler_params=_cparams("arbitrary"),
        name="rmsnorm_cast",
    )(x, g.reshape(1, d))


def _mm_body(*refs, n_a, has_res):
    a_refs, w_ref = refs[:n_a], refs[n_a]
    r_ref = refs[n_a + 1] if has_res else None
    o_ref, wb_ref = refs[-2:]

    @pl.when(pl.program_id(1) == 0)
    def _():
        wb_ref[...] = w_ref[...].astype(BF16)

    acc, k0 = None, 0
    for a_ref in a_refs:
        k = a_ref.shape[1]
        part = jnp.dot(a_ref[...], wb_ref[k0:k0 + k, :], preferred_element_type=F32)
        acc = part if acc is None else acc + part
        k0 += k
    if has_res:
        acc = acc + r_ref[...]
    o_ref[...] = acc


def matmul(a_list, w, res=None, tm=512, tn=512):
    m = a_list[0].shape[0]
    k, n = w.shape
    assert sum(a.shape[1] for a in a_list) == k
    tm, tn = min(tm, m), min(tn, n)
    in_specs = [pl.BlockSpec((tm, a.shape[1]), lambda j, i: (i, 0)) for a in a_list]
    in_specs.append(pl.BlockSpec((k, tn), lambda j, i: (0, j)))
    args = list(a_list) + [w]
    if res is not None:
        in_specs.append(pl.BlockSpec((tm, tn), lambda j, i: (i, j)))
        args.append(res)
    return pl.pallas_call(
        functools.partial(_mm_body, n_a=len(a_list), has_res=res is not None),
        grid=(n // tn, m // tm),
        in_specs=in_specs,
        out_specs=pl.BlockSpec((tm, tn), lambda j, i: (i, j)),
        out_shape=jax.ShapeDtypeStruct((m, n), F32),
        scratch_shapes=[pltpu.VMEM((k, tn), BF16)],
        compiler_params=_cparams("arbitrary", "arbitrary"),
        name="matmul",
    )(*args)


def _online_softmax_step(s, v, m_ref, l_ref, acc_ref):
    m_old = m_ref[...]
    m_new = jnp.maximum(m_old, jnp.max(s, axis=-1, keepdims=True))
    alpha = jnp.exp2(m_old - m_new)
    p = jnp.exp2(s - m_new)
    l_ref[...] = alpha * l_ref[...] + jnp.sum(p, axis=-1, keepdims=True)
    acc_ref[...] = alpha * acc_ref[...] + jnp.dot(p.astype(BF16), v, preferred_element_type=F32)
    m_ref[...] = m_new


def _reset_online(m_ref, l_ref, acc_ref):
    m_ref[...] = jnp.full(m_ref.shape, -jnp.inf, F32)
    l_ref[...] = jnp.zeros(l_ref.shape, F32)
    acc_ref[...] = jnp.zeros(acc_ref.shape, F32)


def _rope(x, cos, sin_signed):
    lane = lax.broadcasted_iota(jnp.int32, x.shape, 1)
    first = (lane % (HEAD_DIM // 2)) < (HEAD_DIM // 4)
    xr = jnp.where(first, pltpu.roll(x, HEAD_DIM - HEAD_DIM // 4, 1), pltpu.roll(x, HEAD_DIM // 4, 1))
    return x * cos + xr * sin_signed


def _score_bound(g_q, g_k, extra=0.0):
    return HEAD_DIM * ATTN_SCALE * BF16_NORM_SLACK * jnp.max(jnp.abs(g_q)) * jnp.max(jnp.abs(g_k)) + extra


def _bounded_flag(bound):
    return (bound <= SAFE_SCORE_BOUND).astype(jnp.int32).reshape(1)


def _diff_attn_body(flag_ref, tab_ref, q_ref, k_ref, v_ref, bkt_ref, gq_ref, gk_ref, gs_ref, lam_ref, o_ref,
                    kn_ref, vb_ref, bias_ref, q_scr, p_ref, lsum_ref, m_ref, l_ref, acc_ref,
                    *, t, n_kt, lam_init):
    hi = pl.program_id(1)
    qi = pl.program_id(2)
    d = HEAD_DIM

    @pl.when(qi == 0)
    def _():
        for j in range(2):
            kj = k_ref[0, :, j * d:(j + 1) * d]
            kn_ref[:, j * d:(j + 1) * d] = _rms(kj, gk_ref[...]).astype(BF16)
        vb_ref[...] = v_ref[0].astype(BF16)

        def build(ti, c):
            bk = bkt_ref[ti]
            acc = jnp.zeros((t, t), F32)
            for b in range(REL_BUCKETS):
                acc = jnp.where(bk == b, tab_ref[b, hi], acc)
            bias_ref[ti] = acc * LOG2E
            return c
        lax.fori_loop(0, 5, build, 0)

    for j in range(2):
        qj = q_ref[0, :, j * d:(j + 1) * d]
        q_scr[j] = (_rms(qj, gq_ref[...]) * (ATTN_SCALE * LOG2E)).astype(BF16)

    def finish(o1, o2):
        lam_v = lam_ref[...]
        lam = (jnp.exp(jnp.sum(lam_v[0:1] * lam_v[1:2], axis=-1, keepdims=True))
               - jnp.exp(jnp.sum(lam_v[2:3] * lam_v[3:4], axis=-1, keepdims=True)) + lam_init)
        o = o1 - lam * o2
        o_ref[0] = (_rms(o, gs_ref[...]) * (1.0 - lam_init)).astype(o_ref.dtype)

    def tile_scores(kt, j):
        k0 = pl.multiple_of(kt * t, t)
        kjt = kn_ref[pl.ds(k0, t), j * d:(j + 1) * d]
        bias = bias_ref[jnp.clip(kt - qi, -2, 2) + 2]
        return lax.dot_general(q_scr[j], kjt, NT_DIMS, preferred_element_type=F32) + bias

    @pl.when(flag_ref[0] == 1)
    def _():
        lsum_ref[...] = jnp.zeros(lsum_ref.shape, F32)

        def step(kt, c):
            k0 = pl.multiple_of(kt * t, t)
            for j in range(2):
                p = jnp.exp2(tile_scores(kt, j))
                p_ref[j, :, pl.ds(k0, t)] = p.astype(BF16)
                part = p[:, 0:LANES]
                for c0 in range(LANES, t, LANES):
                    part = part + p[:, c0:c0 + LANES]
                lsum_ref[j] += part
            return c
        lax.fori_loop(0, n_kt, step, 0)
        outs = []
        for j in range(2):
            l = jnp.sum(lsum_ref[j], axis=-1, keepdims=True)
            outs.append(jnp.dot(p_ref[j], vb_ref[...], preferred_element_type=F32) / l)
        finish(outs[0], outs[1])

    @pl.when(flag_ref[0] != 1)
    def _():
        _reset_online(m_ref, l_ref, acc_ref)

        def step(kt, c):
            k0 = pl.multiple_of(kt * t, t)
            vt = vb_ref[pl.ds(k0, t), :]
            for j in range(2):
                _online_softmax_step(tile_scores(kt, j), vt, m_ref.at[j], l_ref.at[j], acc_ref.at[j])
            return c
        lax.fori_loop(0, n_kt, step, 0)
        finish(acc_ref[0] / l_ref[0], acc_ref[1] / l_ref[1])


def diff_attention(proj, bounded, rel_table, bucket_tiles, g_qa, g_ka, g_subln, lam_vecs, lam_init, t):
    b, s, _ = proj.shape
    h = N_HEADS_A
    w = 2 * HEAD_DIM
    n_kt = s // t
    return pl.pallas_call(
        functools.partial(_diff_attn_body, t=t, n_kt=n_kt, lam_init=lam_init),
        grid=(b, h, n_kt),
        in_specs=[
            _smem_spec(),
            _smem_spec(),
            pl.BlockSpec((1, t, w), lambda bi, hi, qi: (bi, qi, hi)),
            pl.BlockSpec((1, s, w), lambda bi, hi, qi: (bi, 0, h + hi)),
            pl.BlockSpec((1, s, w), lambda bi, hi, qi: (bi, 0, 2 * h + hi)),
            pl.BlockSpec((5, t, t), lambda bi, hi, qi: (0, 0, 0)),
            pl.BlockSpec((1, HEAD_DIM), lambda bi, hi, qi: (0, 0)),
            pl.BlockSpec((1, HEAD_DIM), lambda bi, hi, qi: (0, 0)),
            pl.BlockSpec((1, w), lambda bi, hi, qi: (0, 0)),
            pl.BlockSpec((4, HEAD_DIM), lambda bi, hi, qi: (0, 0)),
        ],
        out_specs=pl.BlockSpec((1, t, w), lambda bi, hi, qi: (bi, qi, hi)),
        out_shape=jax.ShapeDtypeStruct((b, s, h * w), BF16),
        scratch_shapes=[
            pltpu.VMEM((s, w), BF16), pltpu.VMEM((s, w), BF16), pltpu.VMEM((5, t, t), F32),
            pltpu.VMEM((2, t, HEAD_DIM), BF16), pltpu.VMEM((2, t, s), BF16), pltpu.VMEM((2, t, LANES), F32),
            pltpu.VMEM((2, t, 1), F32), pltpu.VMEM((2, t, 1), F32), pltpu.VMEM((2, t, w), F32),
        ],
        compiler_params=_cparams("arbitrary", "arbitrary", "arbitrary"),
        name="diff_attention",
    )(bounded, rel_table, proj, proj, proj, bucket_tiles, g_qa, g_ka, g_subln, lam_vecs)


def _gqa_body(flag_ref, q_ref, k_ref, v_ref, cq_ref, sq_ref, ck_ref, sk_ref, gq_ref, gk_ref, o_ref,
              kn_ref, va_ref, q_scr, p_ref, m_ref, l_ref, acc_ref, *, tq, tk, n_kt, rep):
    qi = pl.program_id(2)
    d = HEAD_DIM

    @pl.when(qi == 0)
    def _():
        kn = _rope(_rms(k_ref[0], gk_ref[...]), ck_ref[...], sk_ref[...])
        kn_ref[...] = kn.astype(BF16)
        va_ref[:, 0:d] = v_ref[0].astype(BF16)
        lane = lax.broadcasted_iota(jnp.int32, (va_ref.shape[0], d), 1)
        va_ref[:, d:2 * d] = jnp.where(lane == 0, 1.0, 0.0).astype(BF16)

    cq, sq = cq_ref[...], sq_ref[...]
    for r in range(rep):
        qr = _rope(_rms(q_ref[0, :, r * d:(r + 1) * d], gq_ref[...]), cq, sq)
        q_scr[r * tq:(r + 1) * tq, :] = (qr * (ATTN_SCALE * LOG2E)).astype(BF16)

    def tile_scores(kt):
        k0 = pl.multiple_of(kt * tk, tk)
        return lax.dot_general(q_scr[...], kn_ref[pl.ds(k0, tk), :], NT_DIMS, preferred_element_type=F32)

    def write(o):
        for r in range(rep):
            o_ref[0, :, r * d:(r + 1) * d] = o[r * tq:(r + 1) * tq].astype(o_ref.dtype)

    @pl.when(flag_ref[0] == 1)
    def _():
        def step(kt, c):
            k0 = pl.multiple_of(kt * tk, tk)
            p_ref[:, pl.ds(k0, tk)] = jnp.exp2(tile_scores(kt)).astype(BF16)
            return c
        lax.fori_loop(0, n_kt, step, 0)
        o = jnp.dot(p_ref[...], va_ref[...], preferred_element_type=F32)
        write(o[:, 0:d] / o[:, d:d + 1])

    @pl.when(flag_ref[0] != 1)
    def _():
        _reset_online(m_ref, l_ref, acc_ref)

        def step(kt, c):
            k0 = pl.multiple_of(kt * tk, tk)
            _online_softmax_step(tile_scores(kt), va_ref[pl.ds(k0, tk), 0:d], m_ref, l_ref, acc_ref)
            return c
        lax.fori_loop(0, n_kt, step, 0)
        write(acc_ref[...] / l_ref[...])


def gqa_attention(proj, bounded, cos, sin_signed, g_qb, g_kb, tq=256, tk=256):
    b, s, _ = proj.shape
    g = N_KV_B
    rep = N_HEADS_B // N_KV_B
    d = HEAD_DIM
    qw = rep * d
    tq, tk = min(tq, s), min(tk, s)
    a_w = N_HEADS_A * 2 * d
    q_blk0 = 3 * a_w // qw
    k_blk0 = (3 * a_w + N_HEADS_B * d) // d
    v_blk0 = k_blk0 + g
    rows = rep * tq
    return pl.pallas_call(
        functools.partial(_gqa_body, tq=tq, tk=tk, n_kt=s // tk, rep=rep),
        grid=(b, g, s // tq),
        in_specs=[
            _smem_spec(),
            pl.BlockSpec((1, tq, qw), lambda bi, gi, qi: (bi, qi, q_blk0 + gi)),
            pl.BlockSpec((1, s, d), lambda bi, gi, qi: (bi, 0, k_blk0 + gi)),
            pl.BlockSpec((1, s, d), lambda bi, gi, qi: (bi, 0, v_blk0 + gi)),
            pl.BlockSpec((tq, d), lambda bi, gi, qi: (qi, 0)),
            pl.BlockSpec((tq, d), lambda bi, gi, qi: (qi, 0)),
            pl.BlockSpec((s, d), lambda bi, gi, qi: (0, 0)),
            pl.BlockSpec((s, d), lambda bi, gi, qi: (0, 0)),
            pl.BlockSpec((1, d), lambda bi, gi, qi: (0, 0)),
            pl.BlockSpec((1, d), lambda bi, gi, qi: (0, 0)),
        ],
        out_specs=pl.BlockSpec((1, tq, qw), lambda bi, gi, qi: (bi, qi, gi)),
        out_shape=jax.ShapeDtypeStruct((b, s, g * qw), BF16),
        scratch_shapes=[
            pltpu.VMEM((s, d), BF16), pltpu.VMEM((s, 2 * d), BF16),
            pltpu.VMEM((rows, d), BF16), pltpu.VMEM((rows, s), BF16),
            pltpu.VMEM((rows, 1), F32), pltpu.VMEM((rows, 1), F32), pltpu.VMEM((rows, d), F32),
        ],
        compiler_params=_cparams("arbitrary", "arbitrary", "arbitrary"),
        name="gqa_attention",
    )(bounded, proj, proj, proj, cos, sin_signed, cos, sin_signed, g_qb, g_kb)


def _cross_attn_body(q_ref, k_ref, v_ref, gq_ref, gk_ref, o_ref):
    d = HEAD_DIM
    for hh in range(N_HEADS_X):
        sl = slice(hh * d, (hh + 1) * d)
        qn = (_rms(q_ref[0, :, sl], gq_ref[...]) * ATTN_SCALE).astype(BF16)
        kn = _rms(k_ref[0, :, sl], gk_ref[...]).astype(BF16)
        s = lax.dot_general(qn, kn, NT_DIMS, preferred_element_type=F32)
        p = jnp.exp(s - jnp.max(s, axis=-1, keepdims=True))
        o = jnp.dot(p.astype(BF16), v_ref[0, :, sl].astype(BF16), preferred_element_type=F32)
        o_ref[0, :, sl] = (o / jnp.sum(p, axis=-1, keepdims=True)).astype(o_ref.dtype)


def cross_attention(qx, kx, vx, g_qx, g_kx, tq=512):
    b, s, w = qx.shape
    n_mem = kx.shape[1]
    tq = min(tq, s)
    return pl.pallas_call(
        _cross_attn_body,
        grid=(b, s // tq),
        in_specs=[
            pl.BlockSpec((1, tq, w), lambda bi, qi: (bi, qi, 0)),
            pl.BlockSpec((1, n_mem, w), lambda bi, qi: (bi, 0, 0)),
            pl.BlockSpec((1, n_mem, w), lambda bi, qi: (bi, 0, 0)),
            pl.BlockSpec((1, HEAD_DIM), lambda bi, qi: (0, 0)),
            pl.BlockSpec((1, HEAD_DIM), lambda bi, qi: (0, 0)),
        ],
        out_specs=pl.BlockSpec((1, tq, w), lambda bi, qi: (bi, qi, 0)),
        out_shape=jax.ShapeDtypeStruct((b, s, w), BF16),
        compiler_params=_cparams("arbitrary", "arbitrary"),
        name="cross_attention",
    )(qx, kx, vx, g_qx, g_kx)


def _split_bf16(x):
    hi = x.astype(BF16)
    lo = (x - hi.astype(F32)).astype(BF16)
    return hi, lo


def _ffn_norm_body(x_ref, g_ref, wr_ref, fp_ref, lg_ref):
    f = _rms(x_ref[...], g_ref[...])
    half = f.shape[1] // 2
    fr = f.astype(BF16).astype(F32)
    lo_bits = pltpu.bitcast(fr[:, :half], jnp.uint32) >> 16
    hi_bits = pltpu.bitcast(fr[:, half:], jnp.uint32) & jnp.uint32(0xFFFF0000)
    fp_ref[...] = lo_bits | hi_bits
    f_hi, f_lo = _split_bf16(f)
    w_hi, w_lo = _split_bf16(wr_ref[...])
    lg_ref[...] = (jnp.dot(f_hi, w_hi, preferred_element_type=F32)
                   + jnp.dot(f_hi, w_lo, preferred_element_type=F32)
                   + jnp.dot(f_lo, w_hi, preferred_element_type=F32))


def ffn_norm_router(x, g, w_router, tm=256):
    m, d = x.shape
    e = w_router.shape[1]
    tm = min(tm, m)
    return pl.pallas_call(
        _ffn_norm_body,
        grid=(m // tm,),
        in_specs=[
            pl.BlockSpec((tm, d), lambda i: (i, 0)),
            pl.BlockSpec((1, d), lambda i: (0, 0)),
            pl.BlockSpec((d, e), lambda i: (0, 0)),
        ],
        out_specs=[pl.BlockSpec((tm, d // 2), lambda i: (i, 0)), pl.BlockSpec((tm, e), lambda i: (i, 0))],
        out_shape=[jax.ShapeDtypeStruct((m, d // 2), jnp.uint32), jax.ShapeDtypeStruct((m, e), F32)],
        compiler_params=_cparams("arbitrary"),
        name="ffn_norm_router",
    )(x, g.reshape(1, d), w_router)


def _unpack_rows(p_ref, o_ref, rows=256):
    m, half = p_ref.shape[1], p_ref.shape[2]
    rows = min(rows, m)
    for r0 in range(0, m, rows):
        p = p_ref[0, r0:r0 + rows, :]
        o_ref[r0:r0 + rows, :half] = pltpu.bitcast(p << 16, F32).astype(BF16)
        o_ref[r0:r0 + rows, half:] = pltpu.bitcast(p & jnp.uint32(0xFFFF0000), F32).astype(BF16)


def _select_top_cap(lg_ref, aff_ref, sel_ref, cs_ref, *, n, cap):
    e = lg_ref.shape[0]
    lg = lg_ref[...]
    ex = jnp.exp(lg - jnp.max(lg, axis=0, keepdims=True))
    aff = ex / jnp.sum(ex, axis=0, keepdims=True)
    aff_ref[...] = aff
    bits = pltpu.bitcast(aff, jnp.int32)

    def search(i, prefix):
        cand = prefix | (jnp.int32(1) << (30 - i))
        cnt = jnp.sum(jnp.where(bits >= cand, 1.0, 0.0), axis=1, keepdims=True)
        return jnp.where(cnt >= cap, cand, prefix)

    thr = lax.fori_loop(0, 31, search, jnp.zeros((e, 1), jnp.int32))
    gt = bits > thr
    need = cap - jnp.sum(jnp.where(gt, 1.0, 0.0), axis=1, keepdims=True)

    tri = (lax.broadcasted_iota(jnp.int32, (LANES, LANES), 0)
           <= lax.broadcasted_iota(jnp.int32, (LANES, LANES), 1)).astype(BF16)

    def cumsum_lanes(ref):
        def blk(j, carry):
            c0 = pl.multiple_of(j * LANES, LANES)
            w = jnp.dot(ref[:, pl.ds(c0, LANES)].astype(BF16), tri, preferred_element_type=F32) + carry
            ref[:, pl.ds(c0, LANES)] = w
            return w[:, LANES - 1:LANES]
        lax.fori_loop(0, n // LANES, blk, jnp.zeros((e, 1), F32))

    eq = bits == thr
    cs_ref[...] = jnp.where(eq, 1.0, 0.0)
    cumsum_lanes(cs_ref)
    sel = gt | (eq & (cs_ref[...] <= need))
    sel_ref[...] = jnp.where(sel, 1.0, 0.0)
    cs_ref[...] = sel_ref[...]
    cumsum_lanes(cs_ref)
    cs_ref[...] = cs_ref[...] * sel_ref[...]


def _route_body(lg_ref, idx_ref, gate_ref, aff_ref, sel_ref, cs_ref, aff_rows, cs_rows, *, n, cap, chunk):
    ei = pl.program_id(0)

    @pl.when(ei == 0)
    def _():
        _select_top_cap(lg_ref, aff_ref, sel_ref, cs_ref, n=n, cap=cap)
        for k in range(lg_ref.shape[0]):
            aff_rows[k] = aff_ref[k:k + 1, :]
            cs_rows[k] = cs_ref[k:k + 1, :]

    tok = lax.broadcasted_iota(jnp.int32, (1, n), 1).astype(F32)
    lane = lax.broadcasted_iota(jnp.int32, (chunk, LANES), 1).astype(F32)
    sub = lax.broadcasted_iota(jnp.int32, (chunk, 1), 0).astype(F32)

    def per_chunk(ci, carry):
        s0 = pl.multiple_of(ci * chunk, chunk)
        lo = s0.astype(F32)
        row = cs_rows[ei]
        inside = jnp.abs(row - (lo + 0.5 * (chunk + 1))) < 0.5 * chunk
        t_first = jnp.min(jnp.where(inside, tok, float(n)), axis=1, keepdims=True)[0, 0]
        t_last = jnp.max(jnp.where(inside, tok, -1.0), axis=1, keepdims=True)[0, 0]
        j_lo = t_first.astype(jnp.int32) // LANES
        j_hi = t_last.astype(jnp.int32) // LANES + 1
        want = sub + (lo + 1.0)

        def per_blk(j, accs):
            acc_i, acc_g = accs
            c0 = pl.multiple_of(j * LANES, LANES)
            hit = cs_rows[ei, :, pl.ds(c0, LANES)] == want
            acc_i = acc_i + jnp.where(hit, lane + c0.astype(F32), 0.0)
            acc_g = acc_g + jnp.where(hit, aff_rows[ei, :, pl.ds(c0, LANES)], 0.0)
            return acc_i, acc_g

        z = jnp.zeros((chunk, LANES), F32)
        acc_i, acc_g = lax.fori_loop(j_lo, j_hi, per_blk, (z, z))
        idx_ref[0, pl.ds(s0, chunk), :] = jnp.sum(acc_i, axis=1, keepdims=True).astype(jnp.int32)
        gate_ref[0, pl.ds(s0, chunk), :] = jnp.sum(acc_g, axis=1, keepdims=True)
        return carry

    lax.fori_loop(0, cap // chunk, per_chunk, 0)


def route(logits_t, cap):
    e, n = logits_t.shape
    chunk = min(LANES, cap)
    return pl.pallas_call(
        functools.partial(_route_body, n=n, cap=cap, chunk=chunk),
        grid=(e,),
        in_specs=[pl.BlockSpec((e, n), lambda i: (0, 0))],
        out_specs=[pl.BlockSpec((1, cap, 1), lambda i: (i, 0, 0)), pl.BlockSpec((1, cap, 1), lambda i: (i, 0, 0))],
        out_shape=[jax.ShapeDtypeStruct((e, cap, 1), jnp.int32), jax.ShapeDtypeStruct((e, cap, 1), F32)],
        scratch_shapes=[pltpu.VMEM((e, n), F32), pltpu.VMEM((e, n), F32), pltpu.VMEM((e, n), F32),
                        pltpu.VMEM((e, 1, n), F32), pltpu.VMEM((e, 1, n), F32)],
        compiler_params=_cparams("arbitrary"),
        name="route",
    )(logits_t)


def _row_copy(src_ref, dst_ref, src_row, dst_row, sem):
    return pltpu.make_async_copy(src_ref.at[pl.ds(src_row, 1)], dst_ref.at[pl.ds(dst_row, 1)], sem)


def _gather_body(idx_ref, src_ref, o_ref, sem, *, chunk):
    base = pl.program_id(0) * chunk

    def issue(r, c):
        _row_copy(src_ref, o_ref, idx_ref[base + r], r, sem).start()
        return c

    def drain(r, c):
        _row_copy(src_ref, o_ref, 0, r, sem).wait()
        return c

    lax.fori_loop(0, chunk, issue, 0)
    lax.fori_loop(0, chunk, drain, 0)


def gather_rows(src, idx, chunk=256):
    r = idx.shape[0]
    chunk = min(chunk, r)
    return pl.pallas_call(
        functools.partial(_gather_body, chunk=chunk),
        grid_spec=pltpu.PrefetchScalarGridSpec(
            num_scalar_prefetch=1,
            grid=(r // chunk,),
            in_specs=[pl.BlockSpec(memory_space=pl.ANY)],
            out_specs=pl.BlockSpec((chunk, src.shape[1]), lambda i, idx_ref: (i, 0)),
            scratch_shapes=[pltpu.SemaphoreType.DMA],
        ),
        out_shape=jax.ShapeDtypeStruct((r, src.shape[1]), src.dtype),
        compiler_params=_cparams("arbitrary"),
        name="gather_rows",
    )(idx, src)


def _gate_up_body(xe_ref, wg_ref, wu_ref, o_ref, xb_ref):
    @pl.when(pl.program_id(2) == 0)
    def _():
        _unpack_rows(xe_ref, xb_ref)

    x = xb_ref[...]
    g = jnp.dot(x, wg_ref[0].astype(BF16), preferred_element_type=F32)
    u = jnp.dot(x, wu_ref[0].astype(BF16), preferred_element_type=F32)
    o_ref[0] = (g * (1.0 / (1.0 + jnp.exp(-g))) * u).astype(o_ref.dtype)


def expert_gate_up(xe, w_gate, w_up, tm=1024, tf=256):
    e, c, dh = xe.shape
    d = 2 * dh
    f = w_gate.shape[2]
    tm, tf = min(tm, c), min(tf, f)
    return pl.pallas_call(
        _gate_up_body,
        grid=(e, c // tm, f // tf),
        in_specs=[
            pl.BlockSpec((1, tm, dh), lambda ei, mi, fi: (ei, mi, 0)),
            pl.BlockSpec((1, d, tf), lambda ei, mi, fi: (ei, 0, fi)),
            pl.BlockSpec((1, d, tf), lambda ei, mi, fi: (ei, 0, fi)),
        ],
        out_specs=pl.BlockSpec((1, tm, tf), lambda ei, mi, fi: (ei, mi, fi)),
        out_shape=jax.ShapeDtypeStruct((e, c, f), BF16),
        scratch_shapes=[pltpu.VMEM((tm, d), BF16)],
        compiler_params=_cparams("arbitrary", "arbitrary", "arbitrary"),
        name="expert_gate_up",
    )(xe, w_gate, w_up)


def _down_body(h_ref, w_ref, gate_ref, o_ref, wb_ref):
    @pl.when(pl.program_id(2) == 0)
    def _():
        wb_ref[...] = w_ref[0].astype(BF16)

    o_ref[0] = jnp.dot(h_ref[0], wb_ref[...], preferred_element_type=F32) * gate_ref[0]


def expert_down(hid, w_down, gates, tm=512, tn=512):
    e, c, f = hid.shape
    d = w_down.shape[2]
    tm, tn = min(tm, c), min(tn, d)
    return pl.pallas_call(
        _down_body,
        grid=(e, d // tn, c // tm),
        in_specs=[
            pl.BlockSpec((1, tm, f), lambda ei, ni, mi: (ei, mi, 0)),
            pl.BlockSpec((1, f, tn), lambda ei, ni, mi: (ei, 0, ni)),
            pl.BlockSpec((1, tm, 1), lambda ei, ni, mi: (ei, mi, 0)),
        ],
        out_specs=pl.BlockSpec((1, tm, tn), lambda ei, ni, mi: (ei, mi, ni)),
        out_shape=jax.ShapeDtypeStruct((e, c, d), F32),
        scratch_shapes=[pltpu.VMEM((f, tn), BF16)],
        compiler_params=_cparams("arbitrary", "arbitrary", "arbitrary"),
        name="expert_down",
    )(hid, w_down, gates)


def _combine_body(idx_ref, ye_ref, x_ref, o_ref, buf_ref, sem_in, sem_out, *, rows):
    del x_ref
    base = pl.program_id(0) * rows

    def fetch(r, c):
        _row_copy(o_ref, buf_ref, idx_ref[base + r], r, sem_in).start()
        return c

    def fetch_wait(r, c):
        _row_copy(o_ref, buf_ref, 0, r, sem_in).wait()
        return c

    def put(r, c):
        _row_copy(buf_ref, o_ref, r, idx_ref[base + r], sem_out).start()
        return c

    def put_wait(r, c):
        _row_copy(buf_ref, o_ref, r, 0, sem_out).wait()
        return c

    lax.fori_loop(0, rows, fetch, 0)
    lax.fori_loop(0, rows, fetch_wait, 0)
    buf_ref[...] = buf_ref[...] + ye_ref[...]
    lax.fori_loop(0, rows, put, 0)
    lax.fori_loop(0, rows, put_wait, 0)


def combine(x, ye, idx, rows):
    r, d = ye.shape
    return pl.pallas_call(
        functools.partial(_combine_body, rows=rows),
        grid_spec=pltpu.PrefetchScalarGridSpec(
            num_scalar_prefetch=1,
            grid=(r // rows,),
            in_specs=[pl.BlockSpec((rows, d), lambda i, idx_ref: (i, 0)), pl.BlockSpec(memory_space=pl.ANY)],
            out_specs=pl.BlockSpec(memory_space=pl.ANY),
            scratch_shapes=[pltpu.VMEM((rows, d), F32), pltpu.SemaphoreType.DMA, pltpu.SemaphoreType.DMA],
        ),
        out_shape=jax.ShapeDtypeStruct(x.shape, x.dtype),
        input_output_aliases={2: 0},
        compiler_params=_cparams("arbitrary"),
        name="combine",
    )(idx, ye, x)


def _rel_bucket(rel):
    nb = REL_BUCKETS // 2
    max_exact = nb // 2
    ret = jnp.where(rel > 0, nb, 0)
    n = jnp.abs(rel)
    nf = jnp.maximum(n, 1).astype(F32)
    large = max_exact + (jnp.log(nf / max_exact) / math.log(REL_MAX_DIST / max_exact)
                         * (nb - max_exact)).astype(jnp.int32)
    large = jnp.minimum(large, nb - 1)
    return ret + jnp.where(n < max_exact, n, large)


def _bucket_tiles(t):
    r = np.arange(t)[:, None]
    c = np.arange(t)[None, :]
    rel = np.stack([dt * t + (c - r) for dt in (-2, -1, 0, 1, 2)]).astype(np.int32)
    return _rel_bucket(jnp.asarray(rel)).astype(jnp.int32)


def _axial_rope_tables(s):
    rows = s // GRID_W
    row = jnp.repeat(jnp.arange(rows), GRID_W).astype(F32)
    col = jnp.tile(jnp.arange(GRID_W), rows).astype(F32)
    half = HEAD_DIM // 2
    inv = 1.0 / (ROPE_THETA ** (jnp.arange(0, half, 2, dtype=F32) / half))
    ang_r = row[:, None] * inv
    ang_c = col[:, None] * inv
    ang = jnp.concatenate([ang_r, ang_r, ang_c, ang_c], axis=-1)
    sign = np.where((np.arange(HEAD_DIM) % half) < half // 2, -1.0, 1.0).astype(np.float32)
    return jnp.cos(ang), jnp.sin(ang) * sign


def _layer(x, mem, rel_table, g_mix, w_in, g_qa, g_ka, lam_vecs, g_subln, g_qb, g_kb, w_o, g_cross,
           g_mem, wq_x, wk_x, wv_x, g_qx, g_kx, wo_x, g_ffn, w_router, w_gate, w_up, w_down, lam_init):
    b, s, dm = x.shape
    n = b * s
    n_mem = mem.shape[1]
    x2d = x.reshape(n, dm)
    row = lambda v: v.reshape(1, -1)

    h = rmsnorm_cast(x2d, g_mix)
    proj = matmul([h], w_in).reshape(b, s, -1)
    t_a = min(256, s)
    assert s % t_a == 0 and t_a >= REL_MAX_DIST, "bias tiles need saturated buckets two tiles away"
    bounded_a = _bounded_flag(_score_bound(g_qa, g_ka, jnp.max(jnp.abs(rel_table))))
    oa = diff_attention(proj, bounded_a, rel_table, _bucket_tiles(t_a), row(g_qa), row(g_ka), row(g_subln),
                        lam_vecs, lam_init, t_a)
    cos, sin_signed = _axial_rope_tables(s)
    ob = gqa_attention(proj, _bounded_flag(_score_bound(g_qb, g_kb)), cos, sin_signed, row(g_qb), row(g_kb))
    x1 = matmul([oa.reshape(n, -1), ob.reshape(n, -1)], w_o, res=x2d)

    c = rmsnorm_cast(x1, g_cross)
    m = rmsnorm_cast(mem.reshape(b * n_mem, dm), g_mem)
    qx = matmul([c], wq_x).reshape(b, s, -1)
    kx = matmul([m], wk_x).reshape(b, n_mem, -1)
    vx = matmul([m], wv_x).reshape(b, n_mem, -1)
    ox = cross_attention(qx, kx, vx, row(g_qx), row(g_kx))
    x2 = matmul([ox.reshape(n, -1)], wo_x, res=x1)

    e = w_router.shape[1]
    cap = EC_FACTOR * n // e
    f_packed, logits = ffn_norm_router(x2, g_ffn, w_router)
    idx, gates = route(logits.T, cap)
    idx_flat = idx.reshape(e * cap)
    xe = gather_rows(f_packed, idx_flat).reshape(e, cap, dm // 2)
    hid = expert_gate_up(xe, w_gate, w_up)
    ye = expert_down(hid, w_down, gates)
    out = combine(x2, ye.reshape(e * cap, dm), idx_flat, rows=min(128, cap))
    return out.reshape(b, s, dm)


def kernel(x_prompt, x_sample, mem_prompt, mem_sample, rel_table, g_mix, w_in, g_qa, g_ka, lam_q1, lam_k1, lam_q2, lam_k2, g_subln, g_qb, g_kb, w_o, g_cross, g_mem, wq_x, wk_x, wv_x, g_qx, g_kx, wo_x, g_ffn, w_router, w_gate, w_up, w_down):
    outs = []
    for x, mem in ((x_prompt, mem_prompt), (x_sample, mem_sample)):
        for l in range(g_mix.shape[0]):
            lam_init = 0.8 - 0.6 * math.exp(-0.3 * l)
            lam_vecs = jnp.stack([lam_q1[l], lam_k1[l], lam_q2[l], lam_k2[l]])
            x = _layer(x, mem, rel_table, g_mix[l], w_in[l], g_qa[l], g_ka[l], lam_vecs, g_subln[l], g_qb[l],
                       g_kb[l], w_o[l], g_cross[l], g_mem[l], wq_x[l], wk_x[l], wv_x[l], g_qx[l], g_kx[l],
                       wo_x[l], g_ffn[l], w_router[l], w_gate[l], w_up[l], w_down[l], lam_init)
        outs.append(x)
    return tuple(outs)
```

```python
import functools
import math

import numpy as np
import jax
import jax.numpy as jnp
from jax import lax
from jax.experimental import pallas as pl
from jax.experimental.pallas import tpu as pltpu

F32 = jnp.float32
BF16 = jnp.bfloat16

HEAD_DIM = 128
N_HEADS_A = 8
N_HEADS_B = 16
N_KV_B = 4
N_HEADS_X = 4
EC_FACTOR = 2
GRID_W = 64
REL_BUCKETS = 32
REL_MAX_DIST = 128
ROPE_THETA = 10000.0
EPS = 1e-6
ATTN_SCALE = HEAD_DIM ** -0.5
LOG2E = math.log2(math.e)

V7X_VMEM_BYTES = 64 * 1024 * 1024
VMEM_LIMIT = V7X_VMEM_BYTES - 8 * 1024 * 1024
LANES = 128
SUBLANES = 8

SAFE_SCORE_BOUND = 30.0
BF16_NORM_SLACK = (1.0 + 2.0 ** -8) ** 2
KEY_TILES_PER_TRIP = 4

NT_DIMS = (((1,), (1,)), ((), ()))


def _cparams(*sem):
    return pltpu.CompilerParams(dimension_semantics=sem, vmem_limit_bytes=VMEM_LIMIT)


def _rms(x, g):
    return x * lax.rsqrt(jnp.mean(x * x, axis=-1, keepdims=True) + EPS) * g


def _smem_spec():
    return pl.BlockSpec(memory_space=pltpu.SMEM)


def _rmsnorm_body(x_ref, g_ref, o_ref):
    o_ref[...] = _rms(x_ref[...], g_ref[...]).astype(o_ref.dtype)


def rmsnorm_cast(x, g, tm=256):
    m, d = x.shape
    tm = min(tm, m)
    return pl.pallas_call(
        _rmsnorm_body,
        grid=(m // tm,),
        in_specs=[pl.BlockSpec((tm, d), lambda i: (i, 0)), pl.BlockSpec((1, d), lambda i: (0, 0))],
        out_specs=pl.BlockSpec((tm, d), lambda i: (i, 0)),
        out_shape=jax.ShapeDtypeStruct((m, d), BF16),
        compiler_params=_cparams("arbitrary"),
        name="rmsnorm_cast",
    )(x, g.reshape(1, d))


def _mm_body(*refs, n_a, has_res):
    a_refs, w_ref = refs[:n_a], refs[n_a]
    r_ref = refs[n_a + 1] if has_res else None
    o_ref, wb_ref = refs[-2:]

    @pl.when(pl.program_id(1) == 0)
    def _():
        wb_ref[...] = w_ref[...].astype(BF16)

    acc, k0 = None, 0
    for a_ref in a_refs:
        k = a_ref.shape[1]
        part = jnp.dot(a_ref[...], wb_ref[k0:k0 + k, :], preferred_element_type=F32)
        acc = part if acc is None else acc + part
        k0 += k
    if has_res:
        acc = acc + r_ref[...]
    o_ref[...] = acc


def matmul(a_list, w, res=None, tm=512, tn=512):
    m = a_list[0].shape[0]
    k, n = w.shape
    assert sum(a.shape[1] for a in a_list) == k
    tm, tn = min(tm, m), min(tn, n)
    in_specs = [pl.BlockSpec((tm, a.shape[1]), lambda j, i: (i, 0)) for a in a_list]
    in_specs.append(pl.BlockSpec((k, tn), lambda j, i: (0, j)))
    args = list(a_list) + [w]
    if res is not None:
        in_specs.append(pl.BlockSpec((tm, tn), lambda j, i: (i, j)))
        args.append(res)
    return pl.pallas_call(
        functools.partial(_mm_body, n_a=len(a_list), has_res=res is not None),
        grid=(n // tn, m // tm),
        in_specs=in_specs,
        out_specs=pl.BlockSpec((tm, tn), lambda j, i: (i, j)),
        out_shape=jax.ShapeDtypeStruct((m, n), F32),
        scratch_shapes=[pltpu.VMEM((k, tn), BF16)],
        compiler_params=_cparams("arbitrary", "arbitrary"),
        name="matmul",
    )(*args)


def _online_softmax_step(s, v, m_ref, l_ref, acc_ref):
    m_old = m_ref[...]
    m_new = jnp.maximum(m_old, jnp.max(s, axis=-1, keepdims=True))
    alpha = jnp.exp2(m_old - m_new)
    p = jnp.exp2(s - m_new)
    l_ref[...] = alpha * l_ref[...] + jnp.sum(p, axis=-1, keepdims=True)
    acc_ref[...] = alpha * acc_ref[...] + jnp.dot(p.astype(BF16), v, preferred_element_type=F32)
    m_ref[...] = m_new


def _reset_online(m_ref, l_ref, acc_ref):
    m_ref[...] = jnp.full(m_ref.shape, -jnp.inf, F32)
    l_ref[...] = jnp.zeros(l_ref.shape, F32)
    acc_ref[...] = jnp.zeros(acc_ref.shape, F32)


def _rope(x, cos, sin_signed):
    lane = lax.broadcasted_iota(jnp.int32, x.shape, 1)
    first = (lane % (HEAD_DIM // 2)) < (HEAD_DIM // 4)
    xr = jnp.where(first, pltpu.roll(x, HEAD_DIM - HEAD_DIM // 4, 1), pltpu.roll(x, HEAD_DIM // 4, 1))
    return x * cos + xr * sin_signed


def _score_bound(g_q, g_k, extra=0.0):
    return HEAD_DIM * ATTN_SCALE * BF16_NORM_SLACK * jnp.max(jnp.abs(g_q)) * jnp.max(jnp.abs(g_k)) + extra


def _bounded_flag(bound):
    return (bound <= SAFE_SCORE_BOUND).astype(jnp.int32).reshape(1)


def _diff_attn_body(flag_ref, tab_ref, q_ref, k_ref, v_ref, bkt_ref, gq_ref, gk_ref, gs_ref, lam_ref, o_ref,
                    kn_ref, vb_ref, bias_ref, q_scr, p_ref, lsum_ref, m_ref, l_ref, acc_ref,
                    *, t, n_kt, group, lam_init):
    hi = pl.program_id(1)
    qi = pl.program_id(2)
    d = HEAD_DIM

    @pl.when(qi == 0)
    def _():
        for j in range(2):
            kj = k_ref[0, :, j * d:(j + 1) * d]
            kn_ref[:, j * d:(j + 1) * d] = _rms(kj, gk_ref[...]).astype(BF16)
        vb_ref[...] = v_ref[0].astype(BF16)

        def build(ti, c):
            bk = bkt_ref[ti]
            acc = jnp.zeros((t, t), F32)
            for b in range(REL_BUCKETS):
                acc = jnp.where(bk == b, tab_ref[b, hi], acc)
            bias_ref[ti] = acc * LOG2E
            return c
        lax.fori_loop(0, 5, build, 0)

    for j in range(2):
        qj = q_ref[0, :, j * d:(j + 1) * d]
        q_scr[j] = (_rms(qj, gq_ref[...]) * (ATTN_SCALE * LOG2E)).astype(BF16)

    def finish(o1, o2):
        lam_v = lam_ref[...]
        lam = (jnp.exp(jnp.sum(lam_v[0:1] * lam_v[1:2], axis=-1, keepdims=True))
               - jnp.exp(jnp.sum(lam_v[2:3] * lam_v[3:4], axis=-1, keepdims=True)) + lam_init)
        o = o1 - lam * o2
        o_ref[0] = (_rms(o, gs_ref[...]) * (1.0 - lam_init)).astype(o_ref.dtype)

    def tile_scores(kt, j):
        k0 = pl.multiple_of(kt * t, t)
        kjt = kn_ref[pl.ds(k0, t), j * d:(j + 1) * d]
        bias = bias_ref[jnp.clip(kt - qi, -2, 2) + 2]
        return lax.dot_general(q_scr[j], kjt, NT_DIMS, preferred_element_type=F32) + bias

    @pl.when(flag_ref[0] == 1)
    def _():
        lsum_ref[...] = jnp.zeros(lsum_ref.shape, F32)

        def step(kg, c):
            parts = [None, None]
            for u in range(group):
                kt = kg * group + u
                k0 = pl.multiple_of(kt * t, t)
                for j in range(2):
                    p = jnp.exp2(tile_scores(kt, j))
                    p_ref[j, :, pl.ds(k0, t)] = p.astype(BF16)
                    for c0 in range(0, t, LANES):
                        pc = p[:, c0:c0 + LANES]
                        parts[j] = pc if parts[j] is None else parts[j] + pc
            for j in range(2):
                lsum_ref[j] += parts[j]
            return c
        lax.fori_loop(0, n_kt // group, step, 0)
        outs = []
        for j in range(2):
            l = jnp.sum(lsum_ref[j], axis=-1, keepdims=True)
            outs.append(jnp.dot(p_ref[j], vb_ref[...], preferred_element_type=F32) / l)
        finish(outs[0], outs[1])

    @pl.when(flag_ref[0] != 1)
    def _():
        _reset_online(m_ref, l_ref, acc_ref)

        def step(kt, c):
            k0 = pl.multiple_of(kt * t, t)
            vt = vb_ref[pl.ds(k0, t), :]
            for j in range(2):
                _online_softmax_step(tile_scores(kt, j), vt, m_ref.at[j], l_ref.at[j], acc_ref.at[j])
            return c
        lax.fori_loop(0, n_kt, step, 0)
        finish(acc_ref[0] / l_ref[0], acc_ref[1] / l_ref[1])


def diff_attention(proj, bounded, rel_table, bucket_tiles, g_qa, g_ka, g_subln, lam_vecs, lam_init, t):
    b, s, _ = proj.shape
    h = N_HEADS_A
    w = 2 * HEAD_DIM
    n_kt = s // t
    group = math.gcd(n_kt, KEY_TILES_PER_TRIP)
    return pl.pallas_call(
        functools.partial(_diff_attn_body, t=t, n_kt=n_kt, group=group, lam_init=lam_init),
        grid=(b, h, n_kt),
        in_specs=[
            _smem_spec(),
            _smem_spec(),
            pl.BlockSpec((1, t, w), lambda bi, hi, qi: (bi, qi, hi)),
            pl.BlockSpec((1, s, w), lambda bi, hi, qi: (bi, 0, h + hi)),
            pl.BlockSpec((1, s, w), lambda bi, hi, qi: (bi, 0, 2 * h + hi)),
            pl.BlockSpec((5, t, t), lambda bi, hi, qi: (0, 0, 0)),
            pl.BlockSpec((1, HEAD_DIM), lambda bi, hi, qi: (0, 0)),
            pl.BlockSpec((1, HEAD_DIM), lambda bi, hi, qi: (0, 0)),
            pl.BlockSpec((1, w), lambda bi, hi, qi: (0, 0)),
            pl.BlockSpec((4, HEAD_DIM), lambda bi, hi, qi: (0, 0)),
        ],
        out_specs=pl.BlockSpec((1, t, w), lambda bi, hi, qi: (bi, qi, hi)),
        out_shape=jax.ShapeDtypeStruct((b, s, h * w), BF16),
        scratch_shapes=[
            pltpu.VMEM((s, w), BF16), pltpu.VMEM((s, w), BF16), pltpu.VMEM((5, t, t), F32),
            pltpu.VMEM((2, t, HEAD_DIM), BF16), pltpu.VMEM((2, t, s), BF16), pltpu.VMEM((2, t, LANES), F32),
            pltpu.VMEM((2, t, 1), F32), pltpu.VMEM((2, t, 1), F32), pltpu.VMEM((2, t, w), F32),
        ],
        compiler_params=_cparams("arbitrary", "arbitrary", "arbitrary"),
        name="diff_attention",
    )(bounded, rel_table, proj, proj, proj, bucket_tiles, g_qa, g_ka, g_subln, lam_vecs)


def _gqa_body(flag_ref, q_ref, k_ref, v_ref, cq_ref, sq_ref, ck_ref, sk_ref, gq_ref, gk_ref, o_ref,
              kn_ref, va_ref, q_scr, p_ref, m_ref, l_ref, acc_ref, *, tq, tk, n_kt, group, rep):
    qi = pl.program_id(2)
    d = HEAD_DIM

    @pl.when(qi == 0)
    def _():
        kn = _rope(_rms(k_ref[0], gk_ref[...]), ck_ref[...], sk_ref[...])
        kn_ref[...] = kn.astype(BF16)
        va_ref[:, 0:d] = v_ref[0].astype(BF16)
        lane = lax.broadcasted_iota(jnp.int32, (va_ref.shape[0], d), 1)
        va_ref[:, d:2 * d] = jnp.where(lane == 0, 1.0, 0.0).astype(BF16)

    cq, sq = cq_ref[...], sq_ref[...]
    for r in range(rep):
        qr = _rope(_rms(q_ref[0, :, r * d:(r + 1) * d], gq_ref[...]), cq, sq)
        q_scr[r * tq:(r + 1) * tq, :] = (qr * (ATTN_SCALE * LOG2E)).astype(BF16)

    def tile_scores(kt):
        k0 = pl.multiple_of(kt * tk, tk)
        return lax.dot_general(q_scr[...], kn_ref[pl.ds(k0, tk), :], NT_DIMS, preferred_element_type=F32)

    def write(o):
        for r in range(rep):
            o_ref[0, :, r * d:(r + 1) * d] = o[r * tq:(r + 1) * tq].astype(o_ref.dtype)

    @pl.when(flag_ref[0] == 1)
    def _():
        def step(kg, c):
            for u in range(group):
                kt = kg * group + u
                k0 = pl.multiple_of(kt * tk, tk)
                p_ref[:, pl.ds(k0, tk)] = jnp.exp2(tile_scores(kt)).astype(BF16)
            return c
        lax.fori_loop(0, n_kt // group, step, 0)
        o = jnp.dot(p_ref[...], va_ref[...], preferred_element_type=F32)
        write(o[:, 0:d] / o[:, d:d + 1])

    @pl.when(flag_ref[0] != 1)
    def _():
        _reset_online(m_ref, l_ref, acc_ref)

        def step(kt, c):
            k0 = pl.multiple_of(kt * tk, tk)
            _online_softmax_step(tile_scores(kt), va_ref[pl.ds(k0, tk), 0:d], m_ref, l_ref, acc_ref)
            return c
        lax.fori_loop(0, n_kt, step, 0)
        write(acc_ref[...] / l_ref[...])


def gqa_attention(proj, bounded, cos, sin_signed, g_qb, g_kb, tq=256, tk=256):
    b, s, _ = proj.shape
    g = N_KV_B
    rep = N_HEADS_B // N_KV_B
    d = HEAD_DIM
    qw = rep * d
    tq, tk = min(tq, s), min(tk, s)
    a_w = N_HEADS_A * 2 * d
    q_blk0 = 3 * a_w // qw
    k_blk0 = (3 * a_w + N_HEADS_B * d) // d
    v_blk0 = k_blk0 + g
    rows = rep * tq
    return pl.pallas_call(
        functools.partial(_gqa_body, tq=tq, tk=tk, n_kt=s // tk, group=math.gcd(s // tk, KEY_TILES_PER_TRIP),
                          rep=rep),
        grid=(b, g, s // tq),
        in_specs=[
            _smem_spec(),
            pl.BlockSpec((1, tq, qw), lambda bi, gi, qi: (bi, qi, q_blk0 + gi)),
            pl.BlockSpec((1, s, d), lambda bi, gi, qi: (bi, 0, k_blk0 + gi)),
            pl.BlockSpec((1, s, d), lambda bi, gi, qi: (bi, 0, v_blk0 + gi)),
            pl.BlockSpec((tq, d), lambda bi, gi, qi: (qi, 0)),
            pl.BlockSpec((tq, d), lambda bi, gi, qi: (qi, 0)),
            pl.BlockSpec((s, d), lambda bi, gi, qi: (0, 0)),
            pl.BlockSpec((s, d), lambda bi, gi, qi: (0, 0)),
            pl.BlockSpec((1, d), lambda bi, gi, qi: (0, 0)),
            pl.BlockSpec((1, d), lambda bi, gi, qi: (0, 0)),
        ],
        out_specs=pl.BlockSpec((1, tq, qw), lambda bi, gi, qi: (bi, qi, gi)),
        out_shape=jax.ShapeDtypeStruct((b, s, g * qw), BF16),
        scratch_shapes=[
            pltpu.VMEM((s, d), BF16), pltpu.VMEM((s, 2 * d), BF16),
            pltpu.VMEM((rows, d), BF16), pltpu.VMEM((rows, s), BF16),
            pltpu.VMEM((rows, 1), F32), pltpu.VMEM((rows, 1), F32), pltpu.VMEM((rows, d), F32),
        ],
        compiler_params=_cparams("arbitrary", "arbitrary", "arbitrary"),
        name="gqa_attention",
    )(bounded, proj, proj, proj, cos, sin_signed, cos, sin_signed, g_qb, g_kb)


def _cross_attn_body(q_ref, k_ref, v_ref, gq_ref, gk_ref, o_ref):
    d = HEAD_DIM
    for hh in range(N_HEADS_X):
        sl = slice(hh * d, (hh + 1) * d)
        qn = (_rms(q_ref[0, :, sl], gq_ref[...]) * ATTN_SCALE).astype(BF16)
        kn = _rms(k_ref[0, :, sl], gk_ref[...]).astype(BF16)
        s = lax.dot_general(qn, kn, NT_DIMS, preferred_element_type=F32)
        p = jnp.exp(s - jnp.max(s, axis=-1, keepdims=True))
        o = jnp.dot(p.astype(BF16), v_ref[0, :, sl].astype(BF16), preferred_element_type=F32)
        o_ref[0, :, sl] = (o / jnp.sum(p, axis=-1, keepdims=True)).astype(o_ref.dtype)


def cross_attention(qx, kx, vx, g_qx, g_kx, tq=512):
    b, s, w = qx.shape
    n_mem = kx.shape[1]
    tq = min(tq, s)
    return pl.pallas_call(
        _cross_attn_body,
        grid=(b, s // tq),
        in_specs=[
            pl.BlockSpec((1, tq, w), lambda bi, qi: (bi, qi, 0)),
            pl.BlockSpec((1, n_mem, w), lambda bi, qi: (bi, 0, 0)),
            pl.BlockSpec((1, n_mem, w), lambda bi, qi: (bi, 0, 0)),
            pl.BlockSpec((1, HEAD_DIM), lambda bi, qi: (0, 0)),
            pl.BlockSpec((1, HEAD_DIM), lambda bi, qi: (0, 0)),
        ],
        out_specs=pl.BlockSpec((1, tq, w), lambda bi, qi: (bi, qi, 0)),
        out_shape=jax.ShapeDtypeStruct((b, s, w), BF16),
        compiler_params=_cparams("arbitrary", "arbitrary"),
        name="cross_attention",
    )(qx, kx, vx, g_qx, g_kx)


def _split_bf16(x):
    hi = x.astype(BF16)
    lo = (x - hi.astype(F32)).astype(BF16)
    return hi, lo


def _ffn_norm_body(x_ref, g_ref, wr_ref, fp_ref, lg_ref):
    f = _rms(x_ref[...], g_ref[...])
    half = f.shape[1] // 2
    fr = f.astype(BF16).astype(F32)
    lo_bits = pltpu.bitcast(fr[:, :half], jnp.uint32) >> 16
    hi_bits = pltpu.bitcast(fr[:, half:], jnp.uint32) & jnp.uint32(0xFFFF0000)
    fp_ref[...] = lo_bits | hi_bits
    f_hi, f_lo = _split_bf16(f)
    w_hi, w_lo = _split_bf16(wr_ref[...])
    lg_ref[...] = (jnp.dot(f_hi, w_hi, preferred_element_type=F32)
                   + jnp.dot(f_hi, w_lo, preferred_element_type=F32)
                   + jnp.dot(f_lo, w_hi, preferred_element_type=F32))


def ffn_norm_router(x, g, w_router, tm=256):
    m, d = x.shape
    e = w_router.shape[1]
    tm = min(tm, m)
    return pl.pallas_call(
        _ffn_norm_body,
        grid=(m // tm,),
        in_specs=[
            pl.BlockSpec((tm, d), lambda i: (i, 0)),
            pl.BlockSpec((1, d), lambda i: (0, 0)),
            pl.BlockSpec((d, e), lambda i: (0, 0)),
        ],
        out_specs=[pl.BlockSpec((tm, d // 2), lambda i: (i, 0)), pl.BlockSpec((tm, e), lambda i: (i, 0))],
        out_shape=[jax.ShapeDtypeStruct((m, d // 2), jnp.uint32), jax.ShapeDtypeStruct((m, e), F32)],
        compiler_params=_cparams("arbitrary"),
        name="ffn_norm_router",
    )(x, g.reshape(1, d), w_router)


def _unpack_rows(p_ref, o_ref, rows=256):
    m, half = p_ref.shape
    rows = min(rows, m)
    for r0 in range(0, m, rows):
        p = p_ref[r0:r0 + rows, :]
        o_ref[r0:r0 + rows, :half] = pltpu.bitcast(p << 16, F32).astype(BF16)
        o_ref[r0:r0 + rows, half:] = pltpu.bitcast(p & jnp.uint32(0xFFFF0000), F32).astype(BF16)


def _select_top_cap(lg_ref, aff_ref, sel_ref, cs_ref, *, n, cap):
    e = lg_ref.shape[0]
    lg = lg_ref[...]
    ex = jnp.exp(lg - jnp.max(lg, axis=0, keepdims=True))
    aff = ex / jnp.sum(ex, axis=0, keepdims=True)
    aff_ref[...] = aff
    bits = pltpu.bitcast(aff, jnp.int32)

    def search(i, prefix):
        cand = prefix | (jnp.int32(1) << (30 - i))
        cnt = jnp.sum(jnp.where(bits >= cand, 1.0, 0.0), axis=1, keepdims=True)
        return jnp.where(cnt >= cap, cand, prefix)

    thr = lax.fori_loop(0, 31, search, jnp.zeros((e, 1), jnp.int32))
    gt = bits > thr
    need = cap - jnp.sum(jnp.where(gt, 1.0, 0.0), axis=1, keepdims=True)

    tri = (lax.broadcasted_iota(jnp.int32, (LANES, LANES), 0)
           <= lax.broadcasted_iota(jnp.int32, (LANES, LANES), 1)).astype(BF16)

    def cumsum_lanes(ref):
        def blk(j, carry):
            c0 = pl.multiple_of(j * LANES, LANES)
            w = jnp.dot(ref[:, pl.ds(c0, LANES)].astype(BF16), tri, preferred_element_type=F32) + carry
            ref[:, pl.ds(c0, LANES)] = w
            return w[:, LANES - 1:LANES]
        lax.fori_loop(0, n // LANES, blk, jnp.zeros((e, 1), F32))

    eq = bits == thr
    cs_ref[...] = jnp.where(eq, 1.0, 0.0)
    cumsum_lanes(cs_ref)
    sel = gt | (eq & (cs_ref[...] <= need))
    sel_ref[...] = jnp.where(sel, 1.0, 0.0)
    cs_ref[...] = sel_ref[...]
    cumsum_lanes(cs_ref)
    cs_ref[...] = cs_ref[...] * sel_ref[...]


def _route_body(lg_ref, idx_ref, gate_ref, aff_ref, sel_ref, cs_ref, aff_rows, cs_rows, *, n, cap, chunk):
    ei = pl.program_id(0)

    @pl.when(ei == 0)
    def _():
        _select_top_cap(lg_ref, aff_ref, sel_ref, cs_ref, n=n, cap=cap)
        for k in range(lg_ref.shape[0]):
            aff_rows[k] = aff_ref[k:k + 1, :]
            cs_rows[k] = cs_ref[k:k + 1, :]

    tok = lax.broadcasted_iota(jnp.int32, (1, n), 1).astype(F32)
    lane = lax.broadcasted_iota(jnp.int32, (chunk, LANES), 1).astype(F32)
    sub = lax.broadcasted_iota(jnp.int32, (chunk, 1), 0).astype(F32)

    def per_chunk(ci, carry):
        s0 = pl.multiple_of(ci * chunk, chunk)
        lo = s0.astype(F32)
        row = cs_rows[ei]
        inside = jnp.abs(row - (lo + 0.5 * (chunk + 1))) < 0.5 * chunk
        t_first = jnp.min(jnp.where(inside, tok, float(n)), axis=1, keepdims=True)[0, 0]
        t_last = jnp.max(jnp.where(inside, tok, -1.0), axis=1, keepdims=True)[0, 0]
        j_lo = t_first.astype(jnp.int32) // LANES
        j_hi = t_last.astype(jnp.int32) // LANES + 1
        want = sub + (lo + 1.0)

        def per_blk(j, accs):
            acc_i, acc_g = accs
            c0 = pl.multiple_of(j * LANES, LANES)
            hit = cs_rows[ei, :, pl.ds(c0, LANES)] == want
            acc_i = acc_i + jnp.where(hit, lane + c0.astype(F32), 0.0)
            acc_g = acc_g + jnp.where(hit, aff_rows[ei, :, pl.ds(c0, LANES)], 0.0)
            return acc_i, acc_g

        z = jnp.zeros((chunk, LANES), F32)
        acc_i, acc_g = lax.fori_loop(j_lo, j_hi, per_blk, (z, z))
        idx_ref[0, pl.ds(s0, chunk), :] = jnp.sum(acc_i, axis=1, keepdims=True).astype(jnp.int32)
        gate_ref[0, pl.ds(s0, chunk), :] = jnp.sum(acc_g, axis=1, keepdims=True)
        return carry

    lax.fori_loop(0, cap // chunk, per_chunk, 0)


def route(logits_t, cap):
    e, n = logits_t.shape
    chunk = min(LANES, cap)
    return pl.pallas_call(
        functools.partial(_route_body, n=n, cap=cap, chunk=chunk),
        grid=(e,),
        in_specs=[pl.BlockSpec((e, n), lambda i: (0, 0))],
        out_specs=[pl.BlockSpec((1, cap, 1), lambda i: (i, 0, 0)), pl.BlockSpec((1, cap, 1), lambda i: (i, 0, 0))],
        out_shape=[jax.ShapeDtypeStruct((e, cap, 1), jnp.int32), jax.ShapeDtypeStruct((e, cap, 1), F32)],
        scratch_shapes=[pltpu.VMEM((e, n), F32), pltpu.VMEM((e, n), F32), pltpu.VMEM((e, n), F32),
                        pltpu.VMEM((e, 1, n), F32), pltpu.VMEM((e, 1, n), F32)],
        compiler_params=_cparams("arbitrary"),
        name="route",
    )(logits_t)


def _row_copy(src_ref, dst_ref, src_row, dst_row, sem):
    return pltpu.make_async_copy(src_ref.at[pl.ds(src_row, 1)], dst_ref.at[pl.ds(dst_row, 1)], sem)


def _wait_rows(src_ref, dst_ref, n, sem):
    pltpu.make_async_copy(src_ref.at[pl.ds(0, n)], dst_ref.at[pl.ds(0, n)], sem).wait()


ROW_DMA_UNROLL = 8


def _gate_up_body(idx_ref, f_ref, wg_ref, wu_ref, o_ref, xg_ref, xb_ref, sems, *, tm, n_f, n_tiles):
    fi = pl.program_id(2)
    tile = pl.program_id(0) * pl.num_programs(1) + pl.program_id(1)
    slot = tile % 2
    share = tm // n_f

    def start_row(tile_id, slot_id, r):
        _row_copy(f_ref, xg_ref.at[slot_id], idx_ref[tile_id * tm + r], r, sems.at[slot_id]).start()

    @pl.when((tile == 0) & (fi == 0))
    def _():
        def issue(r, c):
            start_row(0, 0, r)
            return c
        lax.fori_loop(0, tm, issue, 0, unroll=ROW_DMA_UNROLL)

    @pl.when(fi == 0)
    def _():
        _wait_rows(f_ref, xg_ref.at[slot], tm, sems.at[slot])
        _unpack_rows(xg_ref.at[slot], xb_ref)

    nxt = (tile + 1) % n_tiles
    for r in range(share):
        start_row(nxt, 1 - slot, fi * share + r)

    x = xb_ref[...]
    g = jnp.dot(x, wg_ref[0].astype(BF16), preferred_element_type=F32)
    u = jnp.dot(x, wu_ref[0].astype(BF16), preferred_element_type=F32)
    o_ref[0] = (g * (1.0 / (1.0 + jnp.exp(-g))) * u).astype(o_ref.dtype)

    @pl.when((tile == n_tiles - 1) & (fi == n_f - 1))
    def _():
        _wait_rows(f_ref, xg_ref.at[1 - slot], tm, sems.at[1 - slot])


def expert_gate_up(f_packed, idx, e, w_gate, w_up, tm=1024, tf=256):
    c = idx.shape[0] // e
    dh = f_packed.shape[1]
    d = 2 * dh
    f = w_gate.shape[2]
    tm, tf = min(tm, c), min(tf, f)
    n_f = f // tf
    assert c % tm == 0 and tm % n_f == 0
    return pl.pallas_call(
        functools.partial(_gate_up_body, tm=tm, n_f=n_f, n_tiles=e * (c // tm)),
        grid_spec=pltpu.PrefetchScalarGridSpec(
            num_scalar_prefetch=1,
            grid=(e, c // tm, n_f),
            in_specs=[
                pl.BlockSpec(memory_space=pl.ANY),
                pl.BlockSpec((1, d, tf), lambda ei, mi, fi, idx_ref: (ei, 0, fi)),
                pl.BlockSpec((1, d, tf), lambda ei, mi, fi, idx_ref: (ei, 0, fi)),
            ],
            out_specs=pl.BlockSpec((1, tm, tf), lambda ei, mi, fi, idx_ref: (ei, mi, fi)),
            scratch_shapes=[pltpu.VMEM((2, tm, dh), jnp.uint32), pltpu.VMEM((tm, d), BF16),
                            pltpu.SemaphoreType.DMA((2,))],
        ),
        out_shape=jax.ShapeDtypeStruct((e, c, f), BF16),
        compiler_params=_cparams("arbitrary", "arbitrary", "arbitrary"),
        name="expert_gate_up",
    )(idx, f_packed, w_gate, w_up)


def _down_body(h_ref, w_ref, gate_ref, o_ref, wb_ref):
    @pl.when(pl.program_id(2) == 0)
    def _():
        wb_ref[...] = w_ref[0].astype(BF16)

    o_ref[0] = jnp.dot(h_ref[0], wb_ref[...], preferred_element_type=F32) * gate_ref[0]


def expert_down(hid, w_down, gates, tm=1024, tn=512):
    e, c, f = hid.shape
    d = w_down.shape[2]
    tm, tn = min(tm, c), min(tn, d)
    return pl.pallas_call(
        _down_body,
        grid=(e, d // tn, c // tm),
        in_specs=[
            pl.BlockSpec((1, tm, f), lambda ei, ni, mi: (ei, mi, 0)),
            pl.BlockSpec((1, f, tn), lambda ei, ni, mi: (ei, 0, ni)),
            pl.BlockSpec((1, tm, 1), lambda ei, ni, mi: (ei, mi, 0)),
        ],
        out_specs=pl.BlockSpec((1, tm, tn), lambda ei, ni, mi: (ei, mi, ni)),
        out_shape=jax.ShapeDtypeStruct((e, c, d), F32),
        scratch_shapes=[pltpu.VMEM((f, tn), BF16)],
        compiler_params=_cparams("arbitrary", "arbitrary", "arbitrary"),
        name="expert_down",
    )(hid, w_down, gates)


def _combine_body(idx_ref, ye_ref, x_ref, o_ref, buf_ref, sem_in, sem_out, *, rows):
    del x_ref
    base = pl.program_id(0) * rows

    def fetch(r, c):
        _row_copy(o_ref, buf_ref, idx_ref[base + r], r, sem_in).start()
        return c

    def put(r, c):
        _row_copy(buf_ref, o_ref, r, idx_ref[base + r], sem_out).start()
        return c

    lax.fori_loop(0, rows, fetch, 0, unroll=ROW_DMA_UNROLL)
    _wait_rows(o_ref, buf_ref, rows, sem_in)
    buf_ref[...] = buf_ref[...] + ye_ref[...]
    lax.fori_loop(0, rows, put, 0, unroll=ROW_DMA_UNROLL)
    _wait_rows(buf_ref, o_ref, rows, sem_out)


def combine(x, ye, idx, rows):
    r, d = ye.shape
    return pl.pallas_call(
        functools.partial(_combine_body, rows=rows),
        grid_spec=pltpu.PrefetchScalarGridSpec(
            num_scalar_prefetch=1,
            grid=(r // rows,),
            in_specs=[pl.BlockSpec((rows, d), lambda i, idx_ref: (i, 0)), pl.BlockSpec(memory_space=pl.ANY)],
            out_specs=pl.BlockSpec(memory_space=pl.ANY),
            scratch_shapes=[pltpu.VMEM((rows, d), F32), pltpu.SemaphoreType.DMA, pltpu.SemaphoreType.DMA],
        ),
        out_shape=jax.ShapeDtypeStruct(x.shape, x.dtype),
        input_output_aliases={2: 0},
        compiler_params=_cparams("arbitrary"),
        name="combine",
    )(idx, ye, x)


def _rel_bucket(rel):
    nb = REL_BUCKETS // 2
    max_exact = nb // 2
    ret = jnp.where(rel > 0, nb, 0)
    n = jnp.abs(rel)
    nf = jnp.maximum(n, 1).astype(F32)
    large = max_exact + (jnp.log(nf / max_exact) / math.log(REL_MAX_DIST / max_exact)
                         * (nb - max_exact)).astype(jnp.int32)
    large = jnp.minimum(large, nb - 1)
    return ret + jnp.where(n < max_exact, n, large)


def _bucket_tiles(t):
    r = np.arange(t)[:, None]
    c = np.arange(t)[None, :]
    rel = np.stack([dt * t + (c - r) for dt in (-2, -1, 0, 1, 2)]).astype(np.int32)
    return _rel_bucket(jnp.asarray(rel)).astype(jnp.int32)


def _axial_rope_tables(s):
    rows = s // GRID_W
    row = jnp.repeat(jnp.arange(rows), GRID_W).astype(F32)
    col = jnp.tile(jnp.arange(GRID_W), rows).astype(F32)
    half = HEAD_DIM // 2
    inv = 1.0 / (ROPE_THETA ** (jnp.arange(0, half, 2, dtype=F32) / half))
    ang_r = row[:, None] * inv
    ang_c = col[:, None] * inv
    ang = jnp.concatenate([ang_r, ang_r, ang_c, ang_c], axis=-1)
    sign = np.where((np.arange(HEAD_DIM) % half) < half // 2, -1.0, 1.0).astype(np.float32)
    return jnp.cos(ang), jnp.sin(ang) * sign


def _layer(x, mem, rel_table, g_mix, w_in, g_qa, g_ka, lam_vecs, g_subln, g_qb, g_kb, w_o, g_cross,
           g_mem, wq_x, wk_x, wv_x, g_qx, g_kx, wo_x, g_ffn, w_router, w_gate, w_up, w_down, lam_init):
    b, s, dm = x.shape
    n = b * s
    n_mem = mem.shape[1]
    x2d = x.reshape(n, dm)
    row = lambda v: v.reshape(1, -1)

    h = rmsnorm_cast(x2d, g_mix)
    proj = matmul([h], w_in).reshape(b, s, -1)
    t_a = min(256, s)
    assert s % t_a == 0 and t_a >= REL_MAX_DIST, "bias tiles need saturated buckets two tiles away"
    bounded_a = _bounded_flag(_score_bound(g_qa, g_ka, jnp.max(jnp.abs(rel_table))))
    oa = diff_attention(proj, bounded_a, rel_table, _bucket_tiles(t_a), row(g_qa), row(g_ka), row(g_subln),
                        lam_vecs, lam_init, t_a)
    cos, sin_signed = _axial_rope_tables(s)
    ob = gqa_attention(proj, _bounded_flag(_score_bound(g_qb, g_kb)), cos, sin_signed, row(g_qb), row(g_kb))
    x1 = matmul([oa.reshape(n, -1), ob.reshape(n, -1)], w_o, res=x2d)

    c = rmsnorm_cast(x1, g_cross)
    m = rmsnorm_cast(mem.reshape(b * n_mem, dm), g_mem)
    qx = matmul([c], wq_x).reshape(b, s, -1)
    kx = matmul([m], wk_x).reshape(b, n_mem, -1)
    vx = matmul([m], wv_x).reshape(b, n_mem, -1)
    ox = cross_attention(qx, kx, vx, row(g_qx), row(g_kx))
    x2 = matmul([ox.reshape(n, -1)], wo_x, res=x1)

    e = w_router.shape[1]
    cap = EC_FACTOR * n // e
    f_packed, logits = ffn_norm_router(x2, g_ffn, w_router)
    idx, gates = route(logits.T, cap)
    idx_flat = idx.reshape(e * cap)
    hid = expert_gate_up(f_packed, idx_flat, e, w_gate, w_up)
    ye = expert_down(hid, w_down, gates)
    out = combine(x2, ye.reshape(e * cap, dm), idx_flat, rows=min(256, cap))
    return out.reshape(b, s, dm)


def kernel(x_prompt, x_sample, mem_prompt, mem_sample, rel_table, g_mix, w_in, g_qa, g_ka, lam_q1, lam_k1, lam_q2, lam_k2, g_subln, g_qb, g_kb, w_o, g_cross, g_mem, wq_x, wk_x, wv_x, g_qx, g_kx, wo_x, g_ffn, w_router, w_gate, w_up, w_down):
    outs = []
    for x, mem in ((x_prompt, mem_prompt), (x_sample, mem_sample)):
        for l in range(g_mix.shape[0]):
            lam_init = 0.8 - 0.6 * math.exp(-0.3 * l)
            lam_vecs = jnp.stack([lam_q1[l], lam_k1[l], lam_q2[l], lam_k2[l]])
            x = _layer(x, mem, rel_table, g_mix[l], w_in[l], g_qa[l], g_ka[l], lam_vecs, g_subln[l], g_qb[l],
                       g_kb[l], w_o[l], g_cross[l], g_mem[l], wq_x[l], wk_x[l], wv_x[l], g_qx[l], g_kx[l],
                       wo_x[l], g_ffn[l], w_router[l], w_gate[l], w_up[l], w_down[l], lam_init)
        outs.append(x)
    return tuple(outs)
```

```python
import functools
import math

import numpy as np
import jax
import jax.numpy as jnp
from jax import lax
from jax.experimental import pallas as pl
from jax.experimental.pallas import tpu as pltpu

F32 = jnp.float32
BF16 = jnp.bfloat16

HEAD_DIM = 128
N_HEADS_A = 8
N_HEADS_B = 16
N_KV_B = 4
N_HEADS_X = 4
EC_FACTOR = 2
GRID_W = 64
REL_BUCKETS = 32
REL_MAX_DIST = 128
ROPE_THETA = 10000.0
EPS = 1e-6
ATTN_SCALE = HEAD_DIM ** -0.5
LOG2E = math.log2(math.e)

V7X_VMEM_BYTES = 64 * 1024 * 1024
VMEM_LIMIT = V7X_VMEM_BYTES - 8 * 1024 * 1024
LANES = 128
SUBLANES = 8

SAFE_SCORE_BOUND = 30.0
BF16_NORM_SLACK = (1.0 + 2.0 ** -8) ** 2
KEY_TILES_PER_TRIP = 4
DIFF_QUERY_TILES_PER_STEP = 2

NT_DIMS = (((1,), (1,)), ((), ()))


def _cparams(*sem):
    return pltpu.CompilerParams(dimension_semantics=sem, vmem_limit_bytes=VMEM_LIMIT)


def _rms(x, g):
    return x * lax.rsqrt(jnp.mean(x * x, axis=-1, keepdims=True) + EPS) * g


def _smem_spec():
    return pl.BlockSpec(memory_space=pltpu.SMEM)


def _rmsnorm_body(x_ref, g_ref, o_ref):
    o_ref[...] = _rms(x_ref[...], g_ref[...]).astype(o_ref.dtype)


def rmsnorm_cast(x, g, tm=256):
    m, d = x.shape
    tm = min(tm, m)
    return pl.pallas_call(
        _rmsnorm_body,
        grid=(m // tm,),
        in_specs=[pl.BlockSpec((tm, d), lambda i: (i, 0)), pl.BlockSpec((1, d), lambda i: (0, 0))],
        out_specs=pl.BlockSpec((tm, d), lambda i: (i, 0)),
        out_shape=jax.ShapeDtypeStruct((m, d), BF16),
        compiler_params=_cparams("arbitrary"),
        name="rmsnorm_cast",
    )(x, g.reshape(1, d))


def _mm_body(*refs, n_a, has_res):
    a_refs, w_ref = refs[:n_a], refs[n_a]
    r_ref = refs[n_a + 1] if has_res else None
    o_ref, wb_ref = refs[-2:]

    @pl.when(pl.program_id(1) == 0)
    def _():
        wb_ref[...] = w_ref[...].astype(BF16)

    acc, k0 = None, 0
    for a_ref in a_refs:
        k = a_ref.shape[1]
        part = jnp.dot(a_ref[...], wb_ref[k0:k0 + k, :], preferred_element_type=F32)
        acc = part if acc is None else acc + part
        k0 += k
    if has_res:
        acc = acc + r_ref[...]
    o_ref[...] = acc


def matmul(a_list, w, res=None, tm=512, tn=512):
    m = a_list[0].shape[0]
    k, n = w.shape
    assert sum(a.shape[1] for a in a_list) == k
    tm, tn = min(tm, m), min(tn, n)
    in_specs = [pl.BlockSpec((tm, a.shape[1]), lambda j, i: (i, 0)) for a in a_list]
    in_specs.append(pl.BlockSpec((k, tn), lambda j, i: (0, j)))
    args = list(a_list) + [w]
    if res is not None:
        in_specs.append(pl.BlockSpec((tm, tn), lambda j, i: (i, j)))
        args.append(res)
    return pl.pallas_call(
        functools.partial(_mm_body, n_a=len(a_list), has_res=res is not None),
        grid=(n // tn, m // tm),
        in_specs=in_specs,
        out_specs=pl.BlockSpec((tm, tn), lambda j, i: (i, j)),
        out_shape=jax.ShapeDtypeStruct((m, n), F32),
        scratch_shapes=[pltpu.VMEM((k, tn), BF16)],
        compiler_params=_cparams("arbitrary", "arbitrary"),
        name="matmul",
    )(*args)


def _online_softmax_step(s, v, m_ref, l_ref, acc_ref):
    m_old = m_ref[...]
    m_new = jnp.maximum(m_old, jnp.max(s, axis=-1, keepdims=True))
    alpha = jnp.exp2(m_old - m_new)
    p = jnp.exp2(s - m_new)
    l_ref[...] = alpha * l_ref[...] + jnp.sum(p, axis=-1, keepdims=True)
    acc_ref[...] = alpha * acc_ref[...] + jnp.dot(p.astype(BF16), v, preferred_element_type=F32)
    m_ref[...] = m_new


def _reset_online(m_ref, l_ref, acc_ref):
    m_ref[...] = jnp.full(m_ref.shape, -jnp.inf, F32)
    l_ref[...] = jnp.zeros(l_ref.shape, F32)
    acc_ref[...] = jnp.zeros(acc_ref.shape, F32)


def _rope(x, cos, sin_signed):
    lane = lax.broadcasted_iota(jnp.int32, x.shape, 1)
    first = (lane % (HEAD_DIM // 2)) < (HEAD_DIM // 4)
    xr = jnp.where(first, pltpu.roll(x, HEAD_DIM - HEAD_DIM // 4, 1), pltpu.roll(x, HEAD_DIM // 4, 1))
    return x * cos + xr * sin_signed


def _score_bound(g_q, g_k, extra=0.0):
    return HEAD_DIM * ATTN_SCALE * BF16_NORM_SLACK * jnp.max(jnp.abs(g_q)) * jnp.max(jnp.abs(g_k)) + extra


def _bounded_flag(bound):
    return (bound <= SAFE_SCORE_BOUND).astype(jnp.int32).reshape(1)


def _diff_attn_body(flag_ref, tab_ref, q_ref, k_ref, v_ref, bkt_ref, gq_ref, gk_ref, gs_ref, lam_ref, o_ref,
                    kt_ref, vb_ref, bias_ref, q_scr, p_ref, lsum_ref, m_ref, l_ref, acc_ref,
                    *, t, sub, n_kt, group, lam_init):
    hi = pl.program_id(1)
    qi = pl.program_id(2)
    d = HEAD_DIM

    @pl.when(qi == 0)
    def _():
        for j in range(2):
            kj = k_ref[0, :, j * d:(j + 1) * d]
            kt_ref[j] = _rms(kj, gk_ref[...]).T.astype(BF16)
        vb_ref[...] = v_ref[0].astype(BF16)

        def build(ti, c):
            bk = bkt_ref[ti]
            acc = jnp.zeros((t, t), F32)
            for b in range(REL_BUCKETS):
                acc = jnp.where(bk == b, tab_ref[b, hi], acc)
            bias_ref[ti] = acc * LOG2E
            return c
        lax.fori_loop(0, 5, build, 0)

    for j in range(2):
        qj = q_ref[0, :, j * d:(j + 1) * d]
        q_scr[j] = (_rms(qj, gq_ref[...]) * (ATTN_SCALE * LOG2E)).astype(BF16)

    def finish(o1, o2):
        lam_v = lam_ref[...]
        lam = (jnp.exp(jnp.sum(lam_v[0:1] * lam_v[1:2], axis=-1, keepdims=True))
               - jnp.exp(jnp.sum(lam_v[2:3] * lam_v[3:4], axis=-1, keepdims=True)) + lam_init)
        o = o1 - lam * o2
        o_ref[0] = (_rms(o, gs_ref[...]) * (1.0 - lam_init)).astype(o_ref.dtype)

    def tile_scores(kt, j):
        k0 = pl.multiple_of(kt * t, t)
        s = jnp.dot(q_scr[j], kt_ref[j, :, pl.ds(k0, t)], preferred_element_type=F32)
        rows = [s[u * t:(u + 1) * t] + bias_ref[jnp.clip(kt - (qi * sub + u), -2, 2) + 2] for u in range(sub)]
        return rows[0] if sub == 1 else jnp.concatenate(rows, axis=0)

    @pl.when(flag_ref[0] == 1)
    def _():
        lsum_ref[...] = jnp.zeros(lsum_ref.shape, F32)

        def step(kg, c):
            parts = [None, None]
            for u in range(group):
                kt = kg * group + u
                k0 = pl.multiple_of(kt * t, t)
                for j in range(2):
                    p = jnp.exp2(tile_scores(kt, j))
                    p_ref[j, :, pl.ds(k0, t)] = p.astype(BF16)
                    for c0 in range(0, t, LANES):
                        pc = p[:, c0:c0 + LANES]
                        parts[j] = pc if parts[j] is None else parts[j] + pc
            for j in range(2):
                lsum_ref[j] += parts[j]
            return c
        lax.fori_loop(0, n_kt // group, step, 0)
        outs = []
        for j in range(2):
            l = jnp.sum(lsum_ref[j], axis=-1, keepdims=True)
            outs.append(jnp.dot(p_ref[j], vb_ref[...], preferred_element_type=F32) / l)
        finish(outs[0], outs[1])

    @pl.when(flag_ref[0] != 1)
    def _():
        _reset_online(m_ref, l_ref, acc_ref)

        def step(kt, c):
            k0 = pl.multiple_of(kt * t, t)
            vt = vb_ref[pl.ds(k0, t), :]
            for j in range(2):
                _online_softmax_step(tile_scores(kt, j), vt, m_ref.at[j], l_ref.at[j], acc_ref.at[j])
            return c
        lax.fori_loop(0, n_kt, step, 0)
        finish(acc_ref[0] / l_ref[0], acc_ref[1] / l_ref[1])


def diff_attention(proj, bounded, rel_table, bucket_tiles, g_qa, g_ka, g_subln, lam_vecs, lam_init, t):
    b, s, _ = proj.shape
    h = N_HEADS_A
    w = 2 * HEAD_DIM
    n_kt = s // t
    group = math.gcd(n_kt, KEY_TILES_PER_TRIP)
    sub = math.gcd(n_kt, DIFF_QUERY_TILES_PER_STEP)
    tq = sub * t
    return pl.pallas_call(
        functools.partial(_diff_attn_body, t=t, sub=sub, n_kt=n_kt, group=group, lam_init=lam_init),
        grid=(b, h, s // tq),
        in_specs=[
            _smem_spec(),
            _smem_spec(),
            pl.BlockSpec((1, tq, w), lambda bi, hi, qi: (bi, qi, hi)),
            pl.BlockSpec((1, s, w), lambda bi, hi, qi: (bi, 0, h + hi)),
            pl.BlockSpec((1, s, w), lambda bi, hi, qi: (bi, 0, 2 * h + hi)),
            pl.BlockSpec((5, t, t), lambda bi, hi, qi: (0, 0, 0)),
            pl.BlockSpec((1, HEAD_DIM), lambda bi, hi, qi: (0, 0)),
            pl.BlockSpec((1, HEAD_DIM), lambda bi, hi, qi: (0, 0)),
            pl.BlockSpec((1, w), lambda bi, hi, qi: (0, 0)),
            pl.BlockSpec((4, HEAD_DIM), lambda bi, hi, qi: (0, 0)),
        ],
        out_specs=pl.BlockSpec((1, tq, w), lambda bi, hi, qi: (bi, qi, hi)),
        out_shape=jax.ShapeDtypeStruct((b, s, h * w), BF16),
        scratch_shapes=[
            pltpu.VMEM((2, HEAD_DIM, s), BF16), pltpu.VMEM((s, w), BF16), pltpu.VMEM((5, t, t), F32),
            pltpu.VMEM((2, tq, HEAD_DIM), BF16), pltpu.VMEM((2, tq, s), BF16), pltpu.VMEM((2, tq, LANES), F32),
            pltpu.VMEM((2, tq, 1), F32), pltpu.VMEM((2, tq, 1), F32), pltpu.VMEM((2, tq, w), F32),
        ],
        compiler_params=_cparams("arbitrary", "arbitrary", "arbitrary"),
        name="diff_attention",
    )(bounded, rel_table, proj, proj, proj, bucket_tiles, g_qa, g_ka, g_subln, lam_vecs)


def _gqa_body(flag_ref, q_ref, k_ref, v_ref, cq_ref, sq_ref, ck_ref, sk_ref, gq_ref, gk_ref, o_ref,
              kt_ref, va_ref, q_scr, p_ref, m_ref, l_ref, acc_ref, *, tq, tk, n_kt, group, rep):
    qi = pl.program_id(2)
    d = HEAD_DIM

    @pl.when(qi == 0)
    def _():
        kn = _rope(_rms(k_ref[0], gk_ref[...]), ck_ref[...], sk_ref[...])
        kt_ref[...] = kn.T.astype(BF16)
        va_ref[:, 0:d] = v_ref[0].astype(BF16)
        lane = lax.broadcasted_iota(jnp.int32, (va_ref.shape[0], d), 1)
        va_ref[:, d:2 * d] = jnp.where(lane == 0, 1.0, 0.0).astype(BF16)

    cq, sq = cq_ref[...], sq_ref[...]
    for r in range(rep):
        qr = _rope(_rms(q_ref[0, :, r * d:(r + 1) * d], gq_ref[...]), cq, sq)
        q_scr[r * tq:(r + 1) * tq, :] = (qr * (ATTN_SCALE * LOG2E)).astype(BF16)

    def tile_scores(kt):
        k0 = pl.multiple_of(kt * tk, tk)
        return jnp.dot(q_scr[...], kt_ref[:, pl.ds(k0, tk)], preferred_element_type=F32)

    def write(o):
        for r in range(rep):
            o_ref[0, :, r * d:(r + 1) * d] = o[r * tq:(r + 1) * tq].astype(o_ref.dtype)

    @pl.when(flag_ref[0] == 1)
    def _():
        def step(kg, c):
            for u in range(group):
                kt = kg * group + u
                k0 = pl.multiple_of(kt * tk, tk)
                p_ref[:, pl.ds(k0, tk)] = jnp.exp2(tile_scores(kt)).astype(BF16)
            return c
        lax.fori_loop(0, n_kt // group, step, 0)
        o = jnp.dot(p_ref[...], va_ref[...], preferred_element_type=F32)
        write(o[:, 0:d] / o[:, d:d + 1])

    @pl.when(flag_ref[0] != 1)
    def _():
        _reset_online(m_ref, l_ref, acc_ref)

        def step(kt, c):
            k0 = pl.multiple_of(kt * tk, tk)
            _online_softmax_step(tile_scores(kt), va_ref[pl.ds(k0, tk), 0:d], m_ref, l_ref, acc_ref)
            return c
        lax.fori_loop(0, n_kt, step, 0)
        write(acc_ref[...] / l_ref[...])


def gqa_attention(proj, bounded, cos, sin_signed, g_qb, g_kb, tq=256, tk=256):
    b, s, _ = proj.shape
    g = N_KV_B
    rep = N_HEADS_B // N_KV_B
    d = HEAD_DIM
    qw = rep * d
    tq, tk = min(tq, s), min(tk, s)
    a_w = N_HEADS_A * 2 * d
    q_blk0 = 3 * a_w // qw
    k_blk0 = (3 * a_w + N_HEADS_B * d) // d
    v_blk0 = k_blk0 + g
    rows = rep * tq
    return pl.pallas_call(
        functools.partial(_gqa_body, tq=tq, tk=tk, n_kt=s // tk, group=math.gcd(s // tk, KEY_TILES_PER_TRIP),
                          rep=rep),
        grid=(b, g, s // tq),
        in_specs=[
            _smem_spec(),
            pl.BlockSpec((1, tq, qw), lambda bi, gi, qi: (bi, qi, q_blk0 + gi)),
            pl.BlockSpec((1, s, d), lambda bi, gi, qi: (bi, 0, k_blk0 + gi)),
            pl.BlockSpec((1, s, d), lambda bi, gi, qi: (bi, 0, v_blk0 + gi)),
            pl.BlockSpec((tq, d), lambda bi, gi, qi: (qi, 0)),
            pl.BlockSpec((tq, d), lambda bi, gi, qi: (qi, 0)),
            pl.BlockSpec((s, d), lambda bi, gi, qi: (0, 0)),
            pl.BlockSpec((s, d), lambda bi, gi, qi: (0, 0)),
            pl.BlockSpec((1, d), lambda bi, gi, qi: (0, 0)),
            pl.BlockSpec((1, d), lambda bi, gi, qi: (0, 0)),
        ],
        out_specs=pl.BlockSpec((1, tq, qw), lambda bi, gi, qi: (bi, qi, gi)),
        out_shape=jax.ShapeDtypeStruct((b, s, g * qw), BF16),
        scratch_shapes=[
            pltpu.VMEM((d, s), BF16), pltpu.VMEM((s, 2 * d), BF16),
            pltpu.VMEM((rows, d), BF16), pltpu.VMEM((rows, s), BF16),
            pltpu.VMEM((rows, 1), F32), pltpu.VMEM((rows, 1), F32), pltpu.VMEM((rows, d), F32),
        ],
        compiler_params=_cparams("arbitrary", "arbitrary", "arbitrary"),
        name="gqa_attention",
    )(bounded, proj, proj, proj, cos, sin_signed, cos, sin_signed, g_qb, g_kb)


def _cross_heads(qx, k_ref, v_ref, gq, gk):
    d = HEAD_DIM
    outs = []
    for hh in range(N_HEADS_X):
        sl = slice(hh * d, (hh + 1) * d)
        qn = (_rms(qx[:, sl], gq) * ATTN_SCALE).astype(BF16)
        kn = _rms(k_ref[0, :, sl], gk).astype(BF16)
        s = lax.dot_general(qn, kn, NT_DIMS, preferred_element_type=F32)
        p = jnp.exp(s - jnp.max(s, axis=-1, keepdims=True))
        o = jnp.dot(p.astype(BF16), v_ref[0, :, sl].astype(BF16), preferred_element_type=F32)
        outs.append((o / jnp.sum(p, axis=-1, keepdims=True)).astype(BF16))
    return jnp.concatenate(outs, axis=1)


def _split_bf16(x):
    hi = x.astype(BF16)
    lo = (x - hi.astype(F32)).astype(BF16)
    return hi, lo


def _cross_ffn_body(x_ref, k_ref, v_ref, gc_ref, wq_ref, gq_ref, gk_ref, wo_ref, gf_ref, wr_ref,
                    x2_ref, fp_ref, lg_ref):
    x = x_ref[...]
    c = _rms(x, gc_ref[...]).astype(BF16)
    qx = jnp.dot(c, wq_ref[...], preferred_element_type=F32)
    ox = _cross_heads(qx, k_ref, v_ref, gq_ref[...], gk_ref[...])
    x2 = x + jnp.dot(ox, wo_ref[...], preferred_element_type=F32)
    x2_ref[...] = x2
    f = _rms(x2, gf_ref[...])
    half = f.shape[1] // 2
    fr = f.astype(BF16).astype(F32)
    lo_bits = pltpu.bitcast(fr[:, :half], jnp.uint32) >> 16
    hi_bits = pltpu.bitcast(fr[:, half:], jnp.uint32) & jnp.uint32(0xFFFF0000)
    fp_ref[...] = lo_bits | hi_bits
    f_hi, f_lo = _split_bf16(f)
    w_hi, w_lo = _split_bf16(wr_ref[...])
    lg_ref[...] = (jnp.dot(f_hi, w_hi, preferred_element_type=F32)
                   + jnp.dot(f_hi, w_lo, preferred_element_type=F32)
                   + jnp.dot(f_lo, w_hi, preferred_element_type=F32))


def cross_ffn_block(x, kx, vx, g_cross, wq, g_qx, g_kx, wo, g_ffn, w_router, tm=256):
    n, d = x.shape
    b, n_mem, xw = kx.shape
    e = w_router.shape[1]
    tm = min(tm, n // b)
    tiles_per_seq = n // b // tm
    row = lambda i: (i, 0)
    fixed = lambda i: (0, 0)
    mem = lambda i: (i // tiles_per_seq, 0, 0)
    return pl.pallas_call(
        _cross_ffn_body,
        grid=(n // tm,),
        in_specs=[
            pl.BlockSpec((tm, d), row),
            pl.BlockSpec((1, n_mem, xw), mem),
            pl.BlockSpec((1, n_mem, xw), mem),
            pl.BlockSpec((1, d), fixed),
            pl.BlockSpec((d, xw), fixed),
            pl.BlockSpec((1, HEAD_DIM), fixed),
            pl.BlockSpec((1, HEAD_DIM), fixed),
            pl.BlockSpec((xw, d), fixed),
            pl.BlockSpec((1, d), fixed),
            pl.BlockSpec((d, e), fixed),
        ],
        out_specs=[pl.BlockSpec((tm, d), row), pl.BlockSpec((tm, d // 2), row), pl.BlockSpec((tm, e), row)],
        out_shape=[jax.ShapeDtypeStruct((n, d), F32), jax.ShapeDtypeStruct((n, d // 2), jnp.uint32),
                   jax.ShapeDtypeStruct((n, e), F32)],
        compiler_params=_cparams("arbitrary"),
        name="cross_ffn_block",
    )(x, kx, vx, g_cross.reshape(1, d), wq, g_qx, g_kx, wo, g_ffn.reshape(1, d), w_router)


def _unpack_rows(p_ref, o_ref, rows=256):
    m, half = p_ref.shape
    rows = min(rows, m)
    for r0 in range(0, m, rows):
        p = p_ref[r0:r0 + rows, :]
        o_ref[r0:r0 + rows, :half] = pltpu.bitcast(p << 16, F32).astype(BF16)
        o_ref[r0:r0 + rows, half:] = pltpu.bitcast(p & jnp.uint32(0xFFFF0000), F32).astype(BF16)


def _select_top_cap(lg_ref, aff_ref, sel_ref, cs_ref, *, n, cap):
    e = lg_ref.shape[0]
    lg = lg_ref[...]
    ex = jnp.exp(lg - jnp.max(lg, axis=0, keepdims=True))
    aff = ex / jnp.sum(ex, axis=0, keepdims=True)
    aff_ref[...] = aff
    bits = pltpu.bitcast(aff, jnp.int32)

    def search(i, prefix):
        cand = prefix | (jnp.int32(1) << (30 - i))
        cnt = jnp.sum(jnp.where(bits >= cand, 1.0, 0.0), axis=1, keepdims=True)
        return jnp.where(cnt >= cap, cand, prefix)

    thr = lax.fori_loop(0, 31, search, jnp.zeros((e, 1), jnp.int32))
    gt = bits > thr
    need = cap - jnp.sum(jnp.where(gt, 1.0, 0.0), axis=1, keepdims=True)

    tri = (lax.broadcasted_iota(jnp.int32, (LANES, LANES), 0)
           <= lax.broadcasted_iota(jnp.int32, (LANES, LANES), 1)).astype(BF16)

    def cumsum_lanes(ref):
        def blk(j, carry):
            c0 = pl.multiple_of(j * LANES, LANES)
            w = jnp.dot(ref[:, pl.ds(c0, LANES)].astype(BF16), tri, preferred_element_type=F32) + carry
            ref[:, pl.ds(c0, LANES)] = w
            return w[:, LANES - 1:LANES]
        lax.fori_loop(0, n // LANES, blk, jnp.zeros((e, 1), F32))

    eq = bits == thr
    cs_ref[...] = jnp.where(eq, 1.0, 0.0)
    cumsum_lanes(cs_ref)
    sel = gt | (eq & (cs_ref[...] <= need))
    sel_ref[...] = jnp.where(sel, 1.0, 0.0)
    cs_ref[...] = sel_ref[...]
    cumsum_lanes(cs_ref)
    cs_ref[...] = cs_ref[...] * sel_ref[...]


def _route_body(lg_ref, idx_ref, gate_ref, aff_ref, sel_ref, cs_ref, aff_rows, cs_rows, *, n, cap, chunk):
    ei = pl.program_id(0)

    @pl.when(ei == 0)
    def _():
        _select_top_cap(lg_ref, aff_ref, sel_ref, cs_ref, n=n, cap=cap)
        for k in range(lg_ref.shape[0]):
            aff_rows[k] = aff_ref[k:k + 1, :]
            cs_rows[k] = cs_ref[k:k + 1, :]

    tok = lax.broadcasted_iota(jnp.int32, (1, n), 1).astype(F32)
    lane = lax.broadcasted_iota(jnp.int32, (chunk, LANES), 1).astype(F32)
    sub = lax.broadcasted_iota(jnp.int32, (chunk, 1), 0).astype(F32)

    def per_chunk(ci, carry):
        s0 = pl.multiple_of(ci * chunk, chunk)
        lo = s0.astype(F32)
        row = cs_rows[ei]
        inside = jnp.abs(row - (lo + 0.5 * (chunk + 1))) < 0.5 * chunk
        t_first = jnp.min(jnp.where(inside, tok, float(n)), axis=1, keepdims=True)[0, 0]
        t_last = jnp.max(jnp.where(inside, tok, -1.0), axis=1, keepdims=True)[0, 0]
        j_lo = t_first.astype(jnp.int32) // LANES
        j_hi = t_last.astype(jnp.int32) // LANES + 1
        want = sub + (lo + 1.0)

        def per_blk(j, accs):
            acc_i, acc_g = accs
            c0 = pl.multiple_of(j * LANES, LANES)
            hit = cs_rows[ei, :, pl.ds(c0, LANES)] == want
            acc_i = acc_i + jnp.where(hit, lane + c0.astype(F32), 0.0)
            acc_g = acc_g + jnp.where(hit, aff_rows[ei, :, pl.ds(c0, LANES)], 0.0)
            return acc_i, acc_g

        z = jnp.zeros((chunk, LANES), F32)
        acc_i, acc_g = lax.fori_loop(j_lo, j_hi, per_blk, (z, z))
        idx_ref[0, pl.ds(s0, chunk), :] = jnp.sum(acc_i, axis=1, keepdims=True).astype(jnp.int32)
        gate_ref[0, pl.ds(s0, chunk), :] = jnp.sum(acc_g, axis=1, keepdims=True)
        return carry

    lax.fori_loop(0, cap // chunk, per_chunk, 0)


def route(logits_t, cap):
    e, n = logits_t.shape
    chunk = min(LANES, cap)
    return pl.pallas_call(
        functools.partial(_route_body, n=n, cap=cap, chunk=chunk),
        grid=(e,),
        in_specs=[pl.BlockSpec((e, n), lambda i: (0, 0))],
        out_specs=[pl.BlockSpec((1, cap, 1), lambda i: (i, 0, 0)), pl.BlockSpec((1, cap, 1), lambda i: (i, 0, 0))],
        out_shape=[jax.ShapeDtypeStruct((e, cap, 1), jnp.int32), jax.ShapeDtypeStruct((e, cap, 1), F32)],
        scratch_shapes=[pltpu.VMEM((e, n), F32), pltpu.VMEM((e, n), F32), pltpu.VMEM((e, n), F32),
                        pltpu.VMEM((e, 1, n), F32), pltpu.VMEM((e, 1, n), F32)],
        compiler_params=_cparams("arbitrary"),
        name="route",
    )(logits_t)


def _row_copy(src_ref, dst_ref, src_row, dst_row, sem):
    return pltpu.make_async_copy(src_ref.at[pl.ds(src_row, 1)], dst_ref.at[pl.ds(dst_row, 1)], sem)


def _wait_rows(src_ref, dst_ref, n, sem):
    pltpu.make_async_copy(src_ref.at[pl.ds(0, n)], dst_ref.at[pl.ds(0, n)], sem).wait()


ROW_DMA_UNROLL = 8


def _gate_up_body(idx_ref, f_ref, wg_ref, wu_ref, o_ref, xg_ref, xb_ref, sems, *, tm, n_f, n_tiles):
    fi = pl.program_id(2)
    tile = pl.program_id(0) * pl.num_programs(1) + pl.program_id(1)
    slot = tile % 2
    share = tm // n_f

    def start_row(tile_id, slot_id, r):
        _row_copy(f_ref, xg_ref.at[slot_id], idx_ref[tile_id * tm + r], r, sems.at[slot_id]).start()

    @pl.when((tile == 0) & (fi == 0))
    def _():
        def issue(r, c):
            start_row(0, 0, r)
            return c
        lax.fori_loop(0, tm, issue, 0, unroll=ROW_DMA_UNROLL)

    @pl.when(fi == 0)
    def _():
        _wait_rows(f_ref, xg_ref.at[slot], tm, sems.at[slot])
        _unpack_rows(xg_ref.at[slot], xb_ref)

    nxt = (tile + 1) % n_tiles
    for r in range(share):
        start_row(nxt, 1 - slot, fi * share + r)

    x = xb_ref[...]
    g = jnp.dot(x, wg_ref[0].astype(BF16), preferred_element_type=F32)
    u = jnp.dot(x, wu_ref[0].astype(BF16), preferred_element_type=F32)
    o_ref[0] = (g * (1.0 / (1.0 + jnp.exp(-g))) * u).astype(o_ref.dtype)

    @pl.when((tile == n_tiles - 1) & (fi == n_f - 1))
    def _():
        _wait_rows(f_ref, xg_ref.at[1 - slot], tm, sems.at[1 - slot])


def expert_gate_up(f_packed, idx, e, w_gate, w_up, tm=1024, tf=256):
    c = idx.shape[0] // e
    dh = f_packed.shape[1]
    d = 2 * dh
    f = w_gate.shape[2]
    tm, tf = min(tm, c), min(tf, f)
    n_f = f // tf
    assert c % tm == 0 and tm % n_f == 0
    return pl.pallas_call(
        functools.partial(_gate_up_body, tm=tm, n_f=n_f, n_tiles=e * (c // tm)),
        grid_spec=pltpu.PrefetchScalarGridSpec(
            num_scalar_prefetch=1,
            grid=(e, c // tm, n_f),
            in_specs=[
                pl.BlockSpec(memory_space=pl.ANY),
                pl.BlockSpec((1, d, tf), lambda ei, mi, fi, idx_ref: (ei, 0, fi)),
                pl.BlockSpec((1, d, tf), lambda ei, mi, fi, idx_ref: (ei, 0, fi)),
            ],
            out_specs=pl.BlockSpec((1, tm, tf), lambda ei, mi, fi, idx_ref: (ei, mi, fi)),
            scratch_shapes=[pltpu.VMEM((2, tm, dh), jnp.uint32), pltpu.VMEM((tm, d), BF16),
                            pltpu.SemaphoreType.DMA((2,))],
        ),
        out_shape=jax.ShapeDtypeStruct((e, c, f), BF16),
        compiler_params=_cparams("arbitrary", "arbitrary", "arbitrary"),
        name="expert_gate_up",
    )(idx, f_packed, w_gate, w_up)


def _down_body(h_ref, w_ref, gate_ref, o_ref, wb_ref):
    @pl.when(pl.program_id(2) == 0)
    def _():
        wb_ref[...] = w_ref[0].astype(BF16)

    o_ref[0] = jnp.dot(h_ref[0], wb_ref[...], preferred_element_type=F32) * gate_ref[0]


def expert_down(hid, w_down, gates, tm=1024, tn=512):
    e, c, f = hid.shape
    d = w_down.shape[2]
    tm, tn = min(tm, c), min(tn, d)
    return pl.pallas_call(
        _down_body,
        grid=(e, d // tn, c // tm),
        in_specs=[
            pl.BlockSpec((1, tm, f), lambda ei, ni, mi: (ei, mi, 0)),
            pl.BlockSpec((1, f, tn), lambda ei, ni, mi: (ei, 0, ni)),
            pl.BlockSpec((1, tm, 1), lambda ei, ni, mi: (ei, mi, 0)),
        ],
        out_specs=pl.BlockSpec((1, tm, tn), lambda ei, ni, mi: (ei, mi, ni)),
        out_shape=jax.ShapeDtypeStruct((e, c, d), F32),
        scratch_shapes=[pltpu.VMEM((f, tn), BF16)],
        compiler_params=_cparams("arbitrary", "arbitrary", "arbitrary"),
        name="expert_down",
    )(hid, w_down, gates)


def _combine_body(idx_ref, ye_ref, x_ref, o_ref, buf_ref, sem_in, sem_out, *, rows):
    del x_ref
    base = pl.program_id(0) * rows

    def fetch(r, c):
        _row_copy(o_ref, buf_ref, idx_ref[base + r], r, sem_in).start()
        return c

    def put(r, c):
        _row_copy(buf_ref, o_ref, r, idx_ref[base + r], sem_out).start()
        return c

    lax.fori_loop(0, rows, fetch, 0, unroll=ROW_DMA_UNROLL)
    _wait_rows(o_ref, buf_ref, rows, sem_in)
    buf_ref[...] = buf_ref[...] + ye_ref[...]
    lax.fori_loop(0, rows, put, 0, unroll=ROW_DMA_UNROLL)
    _wait_rows(buf_ref, o_ref, rows, sem_out)


def combine(x, ye, idx, rows):
    r, d = ye.shape
    return pl.pallas_call(
        functools.partial(_combine_body, rows=rows),
        grid_spec=pltpu.PrefetchScalarGridSpec(
            num_scalar_prefetch=1,
            grid=(r // rows,),
            in_specs=[pl.BlockSpec((rows, d), lambda i, idx_ref: (i, 0)), pl.BlockSpec(memory_space=pl.ANY)],
            out_specs=pl.BlockSpec(memory_space=pl.ANY),
            scratch_shapes=[pltpu.VMEM((rows, d), F32), pltpu.SemaphoreType.DMA, pltpu.SemaphoreType.DMA],
        ),
        out_shape=jax.ShapeDtypeStruct(x.shape, x.dtype),
        input_output_aliases={2: 0},
        compiler_params=_cparams("arbitrary"),
        name="combine",
    )(idx, ye, x)


def _rel_bucket(rel):
    nb = REL_BUCKETS // 2
    max_exact = nb // 2
    ret = jnp.where(rel > 0, nb, 0)
    n = jnp.abs(rel)
    nf = jnp.maximum(n, 1).astype(F32)
    large = max_exact + (jnp.log(nf / max_exact) / math.log(REL_MAX_DIST / max_exact)
                         * (nb - max_exact)).astype(jnp.int32)
    large = jnp.minimum(large, nb - 1)
    return ret + jnp.where(n < max_exact, n, large)


def _bucket_tiles(t):
    r = np.arange(t)[:, None]
    c = np.arange(t)[None, :]
    rel = np.stack([dt * t + (c - r) for dt in (-2, -1, 0, 1, 2)]).astype(np.int32)
    return _rel_bucket(jnp.asarray(rel)).astype(jnp.int32)


def _axial_rope_tables(s):
    rows = s // GRID_W
    row = jnp.repeat(jnp.arange(rows), GRID_W).astype(F32)
    col = jnp.tile(jnp.arange(GRID_W), rows).astype(F32)
    half = HEAD_DIM // 2
    inv = 1.0 / (ROPE_THETA ** (jnp.arange(0, half, 2, dtype=F32) / half))
    ang_r = row[:, None] * inv
    ang_c = col[:, None] * inv
    ang = jnp.concatenate([ang_r, ang_r, ang_c, ang_c], axis=-1)
    sign = np.where((np.arange(HEAD_DIM) % half) < half // 2, -1.0, 1.0).astype(np.float32)
    return jnp.cos(ang), jnp.sin(ang) * sign


def _layer(x, mem, rel_table, g_mix, w_in, g_qa, g_ka, lam_vecs, g_subln, g_qb, g_kb, w_o, g_cross,
           g_mem, wq_x, wk_x, wv_x, g_qx, g_kx, wo_x, g_ffn, w_router, w_gate, w_up, w_down, lam_init):
    b, s, dm = x.shape
    n = b * s
    n_mem = mem.shape[1]
    x2d = x.reshape(n, dm)
    row = lambda v: v.reshape(1, -1)

    h = rmsnorm_cast(x2d, g_mix)
    proj = matmul([h], w_in).reshape(b, s, -1)
    t_a = min(256, s)
    assert s % t_a == 0 and t_a >= REL_MAX_DIST, "bias tiles need saturated buckets two tiles away"
    bounded_a = _bounded_flag(_score_bound(g_qa, g_ka, jnp.max(jnp.abs(rel_table))))
    oa = diff_attention(proj, bounded_a, rel_table, _bucket_tiles(t_a), row(g_qa), row(g_ka), row(g_subln),
                        lam_vecs, lam_init, t_a)
    cos, sin_signed = _axial_rope_tables(s)
    ob = gqa_attention(proj, _bounded_flag(_score_bound(g_qb, g_kb)), cos, sin_signed, row(g_qb), row(g_kb))
    x1 = matmul([oa.reshape(n, -1), ob.reshape(n, -1)], w_o, res=x2d)

    m = rmsnorm_cast(mem.reshape(b * n_mem, dm), g_mem)
    kx = matmul([m], wk_x).reshape(b, n_mem, -1)
    vx = matmul([m], wv_x).reshape(b, n_mem, -1)
    x2, f_packed, logits = cross_ffn_block(x1, kx, vx, g_cross, wq_x.astype(BF16), row(g_qx), row(g_kx),
                                           wo_x.astype(BF16), g_ffn, w_router)

    e = w_router.shape[1]
    cap = EC_FACTOR * n // e
    idx, gates = route(logits.T, cap)
    idx_flat = idx.reshape(e * cap)
    hid = expert_gate_up(f_packed, idx_flat, e, w_gate, w_up)
    ye = expert_down(hid, w_down, gates)
    out = combine(x2, ye.reshape(e * cap, dm), idx_flat, rows=min(256, cap))
    return out.reshape(b, s, dm)


def kernel(x_prompt, x_sample, mem_prompt, mem_sample, rel_table, g_mix, w_in, g_qa, g_ka, lam_q1, lam_k1, lam_q2, lam_k2, g_subln, g_qb, g_kb, w_o, g_cross, g_mem, wq_x, wk_x, wv_x, g_qx, g_kx, wo_x, g_ffn, w_router, w_gate, w_up, w_down):
    outs = []
    for x, mem in ((x_prompt, mem_prompt), (x_sample, mem_sample)):
        for l in range(g_mix.shape[0]):
            lam_init = 0.8 - 0.6 * math.exp(-0.3 * l)
            lam_vecs = jnp.stack([lam_q1[l], lam_k1[l], lam_q2[l], lam_k2[l]])
            x = _layer(x, mem, rel_table, g_mix[l], w_in[l], g_qa[l], g_ka[l], lam_vecs, g_subln[l], g_qb[l],
                       g_kb[l], w_o[l], g_cross[l], g_mem[l], wq_x[l], wk_x[l], wv_x[l], g_qx[l], g_kx[l],
                       wo_x[l], g_ffn[l], w_router[l], w_gate[l], w_up[l], w_down[l], lam_init)
        outs.append(x)
    return tuple(outs)
```

```python
import functools
import math

import numpy as np
import jax
import jax.numpy as jnp
from jax import lax
from jax.experimental import pallas as pl
from jax.experimental.pallas import tpu as pltpu

F32 = jnp.float32
BF16 = jnp.bfloat16

HEAD_DIM = 128
N_HEADS_A = 8
N_HEADS_B = 16
N_KV_B = 4
N_HEADS_X = 4
EC_FACTOR = 2
GRID_W = 64
REL_BUCKETS = 32
REL_MAX_DIST = 128
ROPE_THETA = 10000.0
EPS = 1e-6
ATTN_SCALE = HEAD_DIM ** -0.5
LOG2E = math.log2(math.e)

V7X_VMEM_BYTES = 64 * 1024 * 1024
VMEM_LIMIT = V7X_VMEM_BYTES - 8 * 1024 * 1024
LANES = 128
SUBLANES = 8

SAFE_SCORE_BOUND = 30.0
BF16_NORM_SLACK = (1.0 + 2.0 ** -8) ** 2
KEY_TILES_PER_TRIP = 8
DIFF_QUERY_TILES_PER_STEP = 2

NT_DIMS = (((1,), (1,)), ((), ()))


def _cparams(*sem):
    return pltpu.CompilerParams(dimension_semantics=sem, vmem_limit_bytes=VMEM_LIMIT)


def _rms(x, g):
    return x * lax.rsqrt(jnp.mean(x * x, axis=-1, keepdims=True) + EPS) * g


def _smem_spec():
    return pl.BlockSpec(memory_space=pltpu.SMEM)


def _rmsnorm_body(x_ref, g_ref, o_ref):
    o_ref[...] = _rms(x_ref[...], g_ref[...]).astype(o_ref.dtype)


def rmsnorm_cast(x, g, tm=256):
    m, d = x.shape
    tm = min(tm, m)
    return pl.pallas_call(
        _rmsnorm_body,
        grid=(m // tm,),
        in_specs=[pl.BlockSpec((tm, d), lambda i: (i, 0)), pl.BlockSpec((1, d), lambda i: (0, 0))],
        out_specs=pl.BlockSpec((tm, d), lambda i: (i, 0)),
        out_shape=jax.ShapeDtypeStruct((m, d), BF16),
        compiler_params=_cparams("arbitrary"),
        name="rmsnorm_cast",
    )(x, g.reshape(1, d))


def _mm_body(*refs, n_a, has_res):
    a_refs, w_ref = refs[:n_a], refs[n_a]
    r_ref = refs[n_a + 1] if has_res else None
    o_ref, wb_ref = refs[-2:]

    @pl.when(pl.program_id(1) == 0)
    def _():
        wb_ref[...] = w_ref[...].astype(BF16)

    acc, k0 = None, 0
    for a_ref in a_refs:
        k = a_ref.shape[1]
        part = jnp.dot(a_ref[...], wb_ref[k0:k0 + k, :], preferred_element_type=F32)
        acc = part if acc is None else acc + part
        k0 += k
    if has_res:
        acc = acc + r_ref[...]
    o_ref[...] = acc


def matmul(a_list, w, res=None, tm=1024, tn=512):
    m = a_list[0].shape[0]
    k, n = w.shape
    assert sum(a.shape[1] for a in a_list) == k
    tm, tn = min(tm, m), min(tn, n)
    in_specs = [pl.BlockSpec((tm, a.shape[1]), lambda j, i: (i, 0)) for a in a_list]
    in_specs.append(pl.BlockSpec((k, tn), lambda j, i: (0, j)))
    args = list(a_list) + [w]
    if res is not None:
        in_specs.append(pl.BlockSpec((tm, tn), lambda j, i: (i, j)))
        args.append(res)
    return pl.pallas_call(
        functools.partial(_mm_body, n_a=len(a_list), has_res=res is not None),
        grid=(n // tn, m // tm),
        in_specs=in_specs,
        out_specs=pl.BlockSpec((tm, tn), lambda j, i: (i, j)),
        out_shape=jax.ShapeDtypeStruct((m, n), F32),
        scratch_shapes=[pltpu.VMEM((k, tn), BF16)],
        compiler_params=_cparams("arbitrary", "arbitrary"),
        name="matmul",
    )(*args)


def _online_softmax_step(s, v, m_ref, l_ref, acc_ref):
    m_old = m_ref[...]
    m_new = jnp.maximum(m_old, jnp.max(s, axis=-1, keepdims=True))
    alpha = jnp.exp2(m_old - m_new)
    p = jnp.exp2(s - m_new)
    l_ref[...] = alpha * l_ref[...] + jnp.sum(p, axis=-1, keepdims=True)
    acc_ref[...] = alpha * acc_ref[...] + jnp.dot(p.astype(BF16), v, preferred_element_type=F32)
    m_ref[...] = m_new


def _reset_online(m_ref, l_ref, acc_ref):
    m_ref[...] = jnp.full(m_ref.shape, -jnp.inf, F32)
    l_ref[...] = jnp.zeros(l_ref.shape, F32)
    acc_ref[...] = jnp.zeros(acc_ref.shape, F32)


def _rope(x, cos, sin_signed):
    lane = lax.broadcasted_iota(jnp.int32, x.shape, 1)
    first = (lane % (HEAD_DIM // 2)) < (HEAD_DIM // 4)
    xr = jnp.where(first, pltpu.roll(x, HEAD_DIM - HEAD_DIM // 4, 1), pltpu.roll(x, HEAD_DIM // 4, 1))
    return x * cos + xr * sin_signed


def _score_bound(g_q, g_k, extra=0.0):
    return HEAD_DIM * ATTN_SCALE * BF16_NORM_SLACK * jnp.max(jnp.abs(g_q)) * jnp.max(jnp.abs(g_k)) + extra


def _bounded_flag(bound):
    return (bound <= SAFE_SCORE_BOUND).astype(jnp.int32).reshape(1)


def _diff_attn_body(flag_ref, tab_ref, q_ref, k_ref, v_ref, bkt_ref, gq_ref, gk_ref, gs_ref, lam_ref, o_ref,
                    kt_ref, vb_ref, bias_ref, q_scr, p_ref, lsum_ref, m_ref, l_ref, acc_ref,
                    *, t, sub, n_kt, group, lam_init):
    hi = pl.program_id(1)
    qi = pl.program_id(2)
    d = HEAD_DIM

    @pl.when(qi == 0)
    def _():
        for j in range(2):
            kj = k_ref[0, :, j * d:(j + 1) * d]
            kt_ref[j] = _rms(kj, gk_ref[...]).T.astype(BF16)
        vb_ref[...] = v_ref[0].astype(BF16)

        def build(ti, c):
            bk = bkt_ref[ti]
            acc = jnp.zeros((t, t), F32)
            for b in range(REL_BUCKETS):
                acc = jnp.where(bk == b, tab_ref[b, hi], acc)
            bias_ref[ti] = acc * LOG2E
            return c
        lax.fori_loop(0, 5, build, 0)

    for j in range(2):
        qj = q_ref[0, :, j * d:(j + 1) * d]
        q_scr[j] = (_rms(qj, gq_ref[...]) * (ATTN_SCALE * LOG2E)).astype(BF16)

    def finish(o1, o2):
        lam_v = lam_ref[...]
        lam = (jnp.exp(jnp.sum(lam_v[0:1] * lam_v[1:2], axis=-1, keepdims=True))
               - jnp.exp(jnp.sum(lam_v[2:3] * lam_v[3:4], axis=-1, keepdims=True)) + lam_init)
        o = o1 - lam * o2
        o_ref[0] = (_rms(o, gs_ref[...]) * (1.0 - lam_init)).astype(o_ref.dtype)

    def tile_scores(kt, j):
        k0 = pl.multiple_of(kt * t, t)
        s = jnp.dot(q_scr[j], kt_ref[j, :, pl.ds(k0, t)], preferred_element_type=F32)
        rows = [s[u * t:(u + 1) * t] + bias_ref[jnp.clip(kt - (qi * sub + u), -2, 2) + 2] for u in range(sub)]
        return rows[0] if sub == 1 else jnp.concatenate(rows, axis=0)

    @pl.when(flag_ref[0] == 1)
    def _():
        lsum_ref[...] = jnp.zeros(lsum_ref.shape, F32)

        def step(kg, c):
            parts = [None, None]
            for u in range(group):
                kt = kg * group + u
                k0 = pl.multiple_of(kt * t, t)
                for j in range(2):
                    p = jnp.exp2(tile_scores(kt, j))
                    p_ref[j, :, pl.ds(k0, t)] = p.astype(BF16)
                    for c0 in range(0, t, LANES):
                        pc = p[:, c0:c0 + LANES]
                        parts[j] = pc if parts[j] is None else parts[j] + pc
            for j in range(2):
                lsum_ref[j] += parts[j]
            return c
        lax.fori_loop(0, n_kt // group, step, 0)
        outs = []
        for j in range(2):
            l = jnp.sum(lsum_ref[j], axis=-1, keepdims=True)
            outs.append(jnp.dot(p_ref[j], vb_ref[...], preferred_element_type=F32) / l)
        finish(outs[0], outs[1])

    @pl.when(flag_ref[0] != 1)
    def _():
        _reset_online(m_ref, l_ref, acc_ref)

        def step(kt, c):
            k0 = pl.multiple_of(kt * t, t)
            vt = vb_ref[pl.ds(k0, t), :]
            for j in range(2):
                _online_softmax_step(tile_scores(kt, j), vt, m_ref.at[j], l_ref.at[j], acc_ref.at[j])
            return c
        lax.fori_loop(0, n_kt, step, 0)
        finish(acc_ref[0] / l_ref[0], acc_ref[1] / l_ref[1])


def diff_attention(proj, bounded, rel_table, bucket_tiles, g_qa, g_ka, g_subln, lam_vecs, lam_init, t):
    b, s, _ = proj.shape
    h = N_HEADS_A
    w = 2 * HEAD_DIM
    n_kt = s // t
    group = math.gcd(n_kt, KEY_TILES_PER_TRIP)
    sub = math.gcd(n_kt, DIFF_QUERY_TILES_PER_STEP)
    tq = sub * t
    return pl.pallas_call(
        functools.partial(_diff_attn_body, t=t, sub=sub, n_kt=n_kt, group=group, lam_init=lam_init),
        grid=(b, h, s // tq),
        in_specs=[
            _smem_spec(),
            _smem_spec(),
            pl.BlockSpec((1, tq, w), lambda bi, hi, qi: (bi, qi, hi)),
            pl.BlockSpec((1, s, w), lambda bi, hi, qi: (bi, 0, h + hi)),
            pl.BlockSpec((1, s, w), lambda bi, hi, qi: (bi, 0, 2 * h + hi)),
            pl.BlockSpec((5, t, t), lambda bi, hi, qi: (0, 0, 0)),
            pl.BlockSpec((1, HEAD_DIM), lambda bi, hi, qi: (0, 0)),
            pl.BlockSpec((1, HEAD_DIM), lambda bi, hi, qi: (0, 0)),
            pl.BlockSpec((1, w), lambda bi, hi, qi: (0, 0)),
            pl.BlockSpec((4, HEAD_DIM), lambda bi, hi, qi: (0, 0)),
        ],
        out_specs=pl.BlockSpec((1, tq, w), lambda bi, hi, qi: (bi, qi, hi)),
        out_shape=jax.ShapeDtypeStruct((b, s, h * w), BF16),
        scratch_shapes=[
            pltpu.VMEM((2, HEAD_DIM, s), BF16), pltpu.VMEM((s, w), BF16), pltpu.VMEM((5, t, t), F32),
            pltpu.VMEM((2, tq, HEAD_DIM), BF16), pltpu.VMEM((2, tq, s), BF16), pltpu.VMEM((2, tq, LANES), F32),
            pltpu.VMEM((2, tq, 1), F32), pltpu.VMEM((2, tq, 1), F32), pltpu.VMEM((2, tq, w), F32),
        ],
        compiler_params=_cparams("arbitrary", "arbitrary", "arbitrary"),
        name="diff_attention",
    )(bounded, rel_table, proj, proj, proj, bucket_tiles, g_qa, g_ka, g_subln, lam_vecs)


def _gqa_body(flag_ref, q_ref, k_ref, v_ref, cq_ref, sq_ref, ck_ref, sk_ref, gq_ref, gk_ref, o_ref,
              kt_ref, va_ref, q_scr, p_ref, m_ref, l_ref, acc_ref, *, tq, tk, n_kt, group, rep):
    qi = pl.program_id(2)
    d = HEAD_DIM

    @pl.when(qi == 0)
    def _():
        kn = _rope(_rms(k_ref[0], gk_ref[...]), ck_ref[...], sk_ref[...])
        kt_ref[...] = kn.T.astype(BF16)
        va_ref[:, 0:d] = v_ref[0].astype(BF16)
        lane = lax.broadcasted_iota(jnp.int32, (va_ref.shape[0], d), 1)
        va_ref[:, d:2 * d] = jnp.where(lane == 0, 1.0, 0.0).astype(BF16)

    cq, sq = cq_ref[...], sq_ref[...]
    for r in range(rep):
        qr = _rope(_rms(q_ref[0, :, r * d:(r + 1) * d], gq_ref[...]), cq, sq)
        q_scr[r * tq:(r + 1) * tq, :] = (qr * (ATTN_SCALE * LOG2E)).astype(BF16)

    def tile_scores(kt):
        k0 = pl.multiple_of(kt * tk, tk)
        return jnp.dot(q_scr[...], kt_ref[:, pl.ds(k0, tk)], preferred_element_type=F32)

    def write(o):
        for r in range(rep):
            o_ref[0, :, r * d:(r + 1) * d] = o[r * tq:(r + 1) * tq].astype(o_ref.dtype)

    @pl.when(flag_ref[0] == 1)
    def _():
        def step(kg, c):
            for u in range(group):
                kt = kg * group + u
                k0 = pl.multiple_of(kt * tk, tk)
                p_ref[:, pl.ds(k0, tk)] = jnp.exp2(tile_scores(kt)).astype(BF16)
            return c
        lax.fori_loop(0, n_kt // group, step, 0)
        o = jnp.dot(p_ref[...], va_ref[...], preferred_element_type=F32)
        write(o[:, 0:d] / o[:, d:d + 1])

    @pl.when(flag_ref[0] != 1)
    def _():
        _reset_online(m_ref, l_ref, acc_ref)

        def step(kt, c):
            k0 = pl.multiple_of(kt * tk, tk)
            _online_softmax_step(tile_scores(kt), va_ref[pl.ds(k0, tk), 0:d], m_ref, l_ref, acc_ref)
            return c
        lax.fori_loop(0, n_kt, step, 0)
        write(acc_ref[...] / l_ref[...])


def gqa_attention(proj, bounded, cos, sin_signed, g_qb, g_kb, tq=256, tk=256):
    b, s, _ = proj.shape
    g = N_KV_B
    rep = N_HEADS_B // N_KV_B
    d = HEAD_DIM
    qw = rep * d
    tq, tk = min(tq, s), min(tk, s)
    a_w = N_HEADS_A * 2 * d
    q_blk0 = 3 * a_w // qw
    k_blk0 = (3 * a_w + N_HEADS_B * d) // d
    v_blk0 = k_blk0 + g
    rows = rep * tq
    return pl.pallas_call(
        functools.partial(_gqa_body, tq=tq, tk=tk, n_kt=s // tk, group=math.gcd(s // tk, KEY_TILES_PER_TRIP),
                          rep=rep),
        grid=(b, g, s // tq),
        in_specs=[
            _smem_spec(),
            pl.BlockSpec((1, tq, qw), lambda bi, gi, qi: (bi, qi, q_blk0 + gi)),
            pl.BlockSpec((1, s, d), lambda bi, gi, qi: (bi, 0, k_blk0 + gi)),
            pl.BlockSpec((1, s, d), lambda bi, gi, qi: (bi, 0, v_blk0 + gi)),
            pl.BlockSpec((tq, d), lambda bi, gi, qi: (qi, 0)),
            pl.BlockSpec((tq, d), lambda bi, gi, qi: (qi, 0)),
            pl.BlockSpec((s, d), lambda bi, gi, qi: (0, 0)),
            pl.BlockSpec((s, d), lambda bi, gi, qi: (0, 0)),
            pl.BlockSpec((1, d), lambda bi, gi, qi: (0, 0)),
            pl.BlockSpec((1, d), lambda bi, gi, qi: (0, 0)),
        ],
        out_specs=pl.BlockSpec((1, tq, qw), lambda bi, gi, qi: (bi, qi, gi)),
        out_shape=jax.ShapeDtypeStruct((b, s, g * qw), BF16),
        scratch_shapes=[
            pltpu.VMEM((d, s), BF16), pltpu.VMEM((s, 2 * d), BF16),
            pltpu.VMEM((rows, d), BF16), pltpu.VMEM((rows, s), BF16),
            pltpu.VMEM((rows, 1), F32), pltpu.VMEM((rows, 1), F32), pltpu.VMEM((rows, d), F32),
        ],
        compiler_params=_cparams("arbitrary", "arbitrary", "arbitrary"),
        name="gqa_attention",
    )(bounded, proj, proj, proj, cos, sin_signed, cos, sin_signed, g_qb, g_kb)


def _cross_heads(qx, k_ref, v_ref, gq, gk):
    d = HEAD_DIM
    outs = []
    for hh in range(N_HEADS_X):
        sl = slice(hh * d, (hh + 1) * d)
        qn = (_rms(qx[:, sl], gq) * ATTN_SCALE).astype(BF16)
        kn = _rms(k_ref[0, :, sl], gk).astype(BF16)
        s = lax.dot_general(qn, kn, NT_DIMS, preferred_element_type=F32)
        p = jnp.exp(s - jnp.max(s, axis=-1, keepdims=True))
        o = jnp.dot(p.astype(BF16), v_ref[0, :, sl].astype(BF16), preferred_element_type=F32)
        outs.append((o / jnp.sum(p, axis=-1, keepdims=True)).astype(BF16))
    return jnp.concatenate(outs, axis=1)


def _split_bf16(x):
    hi = x.astype(BF16)
    lo = (x - hi.astype(F32)).astype(BF16)
    return hi, lo


def _cross_ffn_body(x_ref, k_ref, v_ref, gc_ref, wq_ref, gq_ref, gk_ref, wo_ref, gf_ref, wr_ref,
                    x2_ref, fp_ref, lg_ref):
    x = x_ref[...]
    c = _rms(x, gc_ref[...]).astype(BF16)
    qx = jnp.dot(c, wq_ref[...], preferred_element_type=F32)
    ox = _cross_heads(qx, k_ref, v_ref, gq_ref[...], gk_ref[...])
    x2 = x + jnp.dot(ox, wo_ref[...], preferred_element_type=F32)
    x2_ref[...] = x2
    f = _rms(x2, gf_ref[...])
    half = f.shape[1] // 2
    fr = f.astype(BF16).astype(F32)
    lo_bits = pltpu.bitcast(fr[:, :half], jnp.uint32) >> 16
    hi_bits = pltpu.bitcast(fr[:, half:], jnp.uint32) & jnp.uint32(0xFFFF0000)
    fp_ref[...] = lo_bits | hi_bits
    f_hi, f_lo = _split_bf16(f)
    w_hi, w_lo = _split_bf16(wr_ref[...])
    lg_ref[...] = (jnp.dot(f_hi, w_hi, preferred_element_type=F32)
                   + jnp.dot(f_hi, w_lo, preferred_element_type=F32)
                   + jnp.dot(f_lo, w_hi, preferred_element_type=F32))


def cross_ffn_block(x, kx, vx, g_cross, wq, g_qx, g_kx, wo, g_ffn, w_router, tm=256):
    n, d = x.shape
    b, n_mem, xw = kx.shape
    e = w_router.shape[1]
    tm = min(tm, n // b)
    tiles_per_seq = n // b // tm
    row = lambda i: (i, 0)
    fixed = lambda i: (0, 0)
    mem = lambda i: (i // tiles_per_seq, 0, 0)
    return pl.pallas_call(
        _cross_ffn_body,
        grid=(n // tm,),
        in_specs=[
            pl.BlockSpec((tm, d), row),
            pl.BlockSpec((1, n_mem, xw), mem),
            pl.BlockSpec((1, n_mem, xw), mem),
            pl.BlockSpec((1, d), fixed),
            pl.BlockSpec((d, xw), fixed),
            pl.BlockSpec((1, HEAD_DIM), fixed),
            pl.BlockSpec((1, HEAD_DIM), fixed),
            pl.BlockSpec((xw, d), fixed),
            pl.BlockSpec((1, d), fixed),
            pl.BlockSpec((d, e), fixed),
        ],
        out_specs=[pl.BlockSpec((tm, d), row), pl.BlockSpec((tm, d // 2), row), pl.BlockSpec((tm, e), row)],
        out_shape=[jax.ShapeDtypeStruct((n, d), F32), jax.ShapeDtypeStruct((n, d // 2), jnp.uint32),
                   jax.ShapeDtypeStruct((n, e), F32)],
        compiler_params=_cparams("arbitrary"),
        name="cross_ffn_block",
    )(x, kx, vx, g_cross.reshape(1, d), wq, g_qx, g_kx, wo, g_ffn.reshape(1, d), w_router)


def _unpack_rows(p_ref, o_ref, rows=256):
    m, half = p_ref.shape
    rows = min(rows, m)
    for r0 in range(0, m, rows):
        p = p_ref[r0:r0 + rows, :]
        o_ref[r0:r0 + rows, :half] = pltpu.bitcast(p << 16, F32).astype(BF16)
        o_ref[r0:r0 + rows, half:] = pltpu.bitcast(p & jnp.uint32(0xFFFF0000), F32).astype(BF16)


def _select_top_cap(lg_ref, aff_ref, sel_ref, cs_ref, *, n, cap):
    e = lg_ref.shape[0]
    lg = lg_ref[...]
    ex = jnp.exp(lg - jnp.max(lg, axis=0, keepdims=True))
    aff = ex / jnp.sum(ex, axis=0, keepdims=True)
    aff_ref[...] = aff
    bits = pltpu.bitcast(aff, jnp.int32)

    def search(i, prefix):
        cand = prefix | (jnp.int32(1) << (30 - i))
        cnt = jnp.sum(jnp.where(bits >= cand, 1.0, 0.0), axis=1, keepdims=True)
        return jnp.where(cnt >= cap, cand, prefix)

    thr = lax.fori_loop(0, 31, search, jnp.zeros((e, 1), jnp.int32))
    gt = bits > thr
    need = cap - jnp.sum(jnp.where(gt, 1.0, 0.0), axis=1, keepdims=True)

    tri = (lax.broadcasted_iota(jnp.int32, (LANES, LANES), 0)
           <= lax.broadcasted_iota(jnp.int32, (LANES, LANES), 1)).astype(BF16)

    def cumsum_lanes(ref):
        def blk(j, carry):
            c0 = pl.multiple_of(j * LANES, LANES)
            w = jnp.dot(ref[:, pl.ds(c0, LANES)].astype(BF16), tri, preferred_element_type=F32) + carry
            ref[:, pl.ds(c0, LANES)] = w
            return w[:, LANES - 1:LANES]
        lax.fori_loop(0, n // LANES, blk, jnp.zeros((e, 1), F32))

    eq = bits == thr
    cs_ref[...] = jnp.where(eq, 1.0, 0.0)
    cumsum_lanes(cs_ref)
    sel = gt | (eq & (cs_ref[...] <= need))
    sel_ref[...] = jnp.where(sel, 1.0, 0.0)
    cs_ref[...] = sel_ref[...]
    cumsum_lanes(cs_ref)
    cs_ref[...] = cs_ref[...] * sel_ref[...]


def _route_body(lg_ref, idx_ref, gate_ref, aff_ref, sel_ref, cs_ref, aff_rows, cs_rows, *, n, cap, chunk):
    ei = pl.program_id(0)

    @pl.when(ei == 0)
    def _():
        _select_top_cap(lg_ref, aff_ref, sel_ref, cs_ref, n=n, cap=cap)
        for k in range(lg_ref.shape[0]):
            aff_rows[k] = aff_ref[k:k + 1, :]
            cs_rows[k] = cs_ref[k:k + 1, :]

    tok = lax.broadcasted_iota(jnp.int32, (1, n), 1).astype(F32)
    lane = lax.broadcasted_iota(jnp.int32, (chunk, LANES), 1).astype(F32)
    sub = lax.broadcasted_iota(jnp.int32, (chunk, 1), 0).astype(F32)

    def per_chunk(ci, carry):
        s0 = pl.multiple_of(ci * chunk, chunk)
        lo = s0.astype(F32)
        row = cs_rows[ei]
        inside = jnp.abs(row - (lo + 0.5 * (chunk + 1))) < 0.5 * chunk
        t_first = jnp.min(jnp.where(inside, tok, float(n)), axis=1, keepdims=True)[0, 0]
        t_last = jnp.max(jnp.where(inside, tok, -1.0), axis=1, keepdims=True)[0, 0]
        j_lo = t_first.astype(jnp.int32) // LANES
        j_hi = t_last.astype(jnp.int32) // LANES + 1
        want = sub + (lo + 1.0)

        def per_blk(j, accs):
            acc_i, acc_g = accs
            c0 = pl.multiple_of(j * LANES, LANES)
            hit = cs_rows[ei, :, pl.ds(c0, LANES)] == want
            acc_i = acc_i + jnp.where(hit, lane + c0.astype(F32), 0.0)
            acc_g = acc_g + jnp.where(hit, aff_rows[ei, :, pl.ds(c0, LANES)], 0.0)
            return acc_i, acc_g

        z = jnp.zeros((chunk, LANES), F32)
        acc_i, acc_g = lax.fori_loop(j_lo, j_hi, per_blk, (z, z))
        idx_ref[0, pl.ds(s0, chunk), :] = jnp.sum(acc_i, axis=1, keepdims=True).astype(jnp.int32)
        gate_ref[0, pl.ds(s0, chunk), :] = jnp.sum(acc_g, axis=1, keepdims=True)
        return carry

    lax.fori_loop(0, cap // chunk, per_chunk, 0)


def route(logits_t, cap):
    e, n = logits_t.shape
    chunk = min(LANES, cap)
    return pl.pallas_call(
        functools.partial(_route_body, n=n, cap=cap, chunk=chunk),
        grid=(e,),
        in_specs=[pl.BlockSpec((e, n), lambda i: (0, 0))],
        out_specs=[pl.BlockSpec((1, cap, 1), lambda i: (i, 0, 0)), pl.BlockSpec((1, cap, 1), lambda i: (i, 0, 0))],
        out_shape=[jax.ShapeDtypeStruct((e, cap, 1), jnp.int32), jax.ShapeDtypeStruct((e, cap, 1), F32)],
        scratch_shapes=[pltpu.VMEM((e, n), F32), pltpu.VMEM((e, n), F32), pltpu.VMEM((e, n), F32),
                        pltpu.VMEM((e, 1, n), F32), pltpu.VMEM((e, 1, n), F32)],
        compiler_params=_cparams("arbitrary"),
        name="route",
    )(logits_t)


def _row_copy(src_ref, dst_ref, src_row, dst_row, sem):
    return pltpu.make_async_copy(src_ref.at[pl.ds(src_row, 1)], dst_ref.at[pl.ds(dst_row, 1)], sem)


def _wait_rows(src_ref, dst_ref, n, sem):
    pltpu.make_async_copy(src_ref.at[pl.ds(0, n)], dst_ref.at[pl.ds(0, n)], sem).wait()


ROW_DMA_UNROLL = 8


def _gate_up_body(idx_ref, f_ref, wg_ref, wu_ref, o_ref, xg_ref, xb_ref, sems, *, tm, n_f, n_tiles):
    fi = pl.program_id(2)
    tile = pl.program_id(0) * pl.num_programs(1) + pl.program_id(1)
    slot = tile % 2
    share = tm // n_f

    def start_row(tile_id, slot_id, r):
        _row_copy(f_ref, xg_ref.at[slot_id], idx_ref[tile_id * tm + r], r, sems.at[slot_id]).start()

    @pl.when((tile == 0) & (fi == 0))
    def _():
        def issue(r, c):
            start_row(0, 0, r)
            return c
        lax.fori_loop(0, tm, issue, 0, unroll=ROW_DMA_UNROLL)

    @pl.when(fi == 0)
    def _():
        _wait_rows(f_ref, xg_ref.at[slot], tm, sems.at[slot])
        _unpack_rows(xg_ref.at[slot], xb_ref)

    nxt = (tile + 1) % n_tiles
    for r in range(share):
        start_row(nxt, 1 - slot, fi * share + r)

    x = xb_ref[...]
    g = jnp.dot(x, wg_ref[0].astype(BF16), preferred_element_type=F32)
    u = jnp.dot(x, wu_ref[0].astype(BF16), preferred_element_type=F32)
    o_ref[0] = (g * (1.0 / (1.0 + jnp.exp(-g))) * u).astype(o_ref.dtype)

    @pl.when((tile == n_tiles - 1) & (fi == n_f - 1))
    def _():
        _wait_rows(f_ref, xg_ref.at[1 - slot], tm, sems.at[1 - slot])


def expert_gate_up(f_packed, idx, e, w_gate, w_up, tm=1024, tf=256):
    c = idx.shape[0] // e
    dh = f_packed.shape[1]
    d = 2 * dh
    f = w_gate.shape[2]
    tm, tf = min(tm, c), min(tf, f)
    n_f = f // tf
    assert c % tm == 0 and tm % n_f == 0
    return pl.pallas_call(
        functools.partial(_gate_up_body, tm=tm, n_f=n_f, n_tiles=e * (c // tm)),
        grid_spec=pltpu.PrefetchScalarGridSpec(
            num_scalar_prefetch=1,
            grid=(e, c // tm, n_f),
            in_specs=[
                pl.BlockSpec(memory_space=pl.ANY),
                pl.BlockSpec((1, d, tf), lambda ei, mi, fi, idx_ref: (ei, 0, fi)),
                pl.BlockSpec((1, d, tf), lambda ei, mi, fi, idx_ref: (ei, 0, fi)),
            ],
            out_specs=pl.BlockSpec((1, tm, tf), lambda ei, mi, fi, idx_ref: (ei, mi, fi)),
            scratch_shapes=[pltpu.VMEM((2, tm, dh), jnp.uint32), pltpu.VMEM((tm, d), BF16),
                            pltpu.SemaphoreType.DMA((2,))],
        ),
        out_shape=jax.ShapeDtypeStruct((e, c, f), BF16),
        compiler_params=_cparams("arbitrary", "arbitrary", "arbitrary"),
        name="expert_gate_up",
    )(idx, f_packed, w_gate, w_up)


def _down_body(h_ref, w_ref, gate_ref, o_ref):
    o_ref[0] = jnp.dot(h_ref[0], w_ref[0].astype(BF16), preferred_element_type=F32) * gate_ref[0]


def expert_down(hid, w_down, gates, tm=1024, tn=512):
    e, c, f = hid.shape
    d = w_down.shape[2]
    tm, tn = min(tm, c), min(tn, d)
    return pl.pallas_call(
        _down_body,
        grid=(e, d // tn, c // tm),
        in_specs=[
            pl.BlockSpec((1, tm, f), lambda ei, ni, mi: (ei, mi, 0)),
            pl.BlockSpec((1, f, tn), lambda ei, ni, mi: (ei, 0, ni)),
            pl.BlockSpec((1, tm, 1), lambda ei, ni, mi: (ei, mi, 0)),
        ],
        out_specs=pl.BlockSpec((1, tm, tn), lambda ei, ni, mi: (ei, mi, ni)),
        out_shape=jax.ShapeDtypeStruct((e, c, d), F32),
        compiler_params=_cparams("arbitrary", "arbitrary", "arbitrary"),
        name="expert_down",
    )(hid, w_down, gates)


COMBINE_SLOTS = 3


def _combine_body(idx_ref, ye_ref, x_ref, o_ref, buf_ref, sem_in, sem_out, *, rows, tpe, n_tiles):
    del x_ref
    i = pl.program_id(0)
    slot = i % COMBINE_SLOTS
    first = (i % tpe) == 0
    last = (i % tpe) == tpe - 1

    def fetch(tile, s):
        def one(r, c):
            _row_copy(o_ref, buf_ref.at[s], idx_ref[tile * rows + r], r, sem_in.at[s]).start()
            return c
        lax.fori_loop(0, rows, one, 0, unroll=ROW_DMA_UNROLL)

    def put(tile, s):
        def one(r, c):
            _row_copy(buf_ref.at[s], o_ref, r, idx_ref[tile * rows + r], sem_out.at[s]).start()
            return c
        lax.fori_loop(0, rows, one, 0, unroll=ROW_DMA_UNROLL)

    def wait_put(s):
        _wait_rows(buf_ref.at[s], o_ref, rows, sem_out.at[s])

    @pl.when((i >= 2) & (((i - 1) % tpe) != 0))
    def _():
        wait_put((i - 2) % COMBINE_SLOTS)

    @pl.when(first)
    def _():
        @pl.when(i >= 1)
        def _():
            wait_put((i - 1) % COMBINE_SLOTS)
        fetch(i, slot)

    @pl.when(jnp.logical_not(last))
    def _():
        fetch(i + 1, (i + 1) % COMBINE_SLOTS)

    _wait_rows(o_ref, buf_ref.at[slot], rows, sem_in.at[slot])
    buf_ref[slot] = buf_ref[slot] + ye_ref[...]
    put(i, slot)

    @pl.when(i == n_tiles - 1)
    def _():
        if n_tiles >= 2 and (n_tiles - 1) % tpe != 0:
            wait_put((n_tiles - 2) % COMBINE_SLOTS)
        wait_put((n_tiles - 1) % COMBINE_SLOTS)


def combine(x, ye, idx, rows, rows_per_expert):
    r, d = ye.shape
    assert rows_per_expert % rows == 0
    return pl.pallas_call(
        functools.partial(_combine_body, rows=rows, tpe=rows_per_expert // rows, n_tiles=r // rows),
        grid_spec=pltpu.PrefetchScalarGridSpec(
            num_scalar_prefetch=1,
            grid=(r // rows,),
            in_specs=[pl.BlockSpec((rows, d), lambda i, idx_ref: (i, 0)), pl.BlockSpec(memory_space=pl.ANY)],
            out_specs=pl.BlockSpec(memory_space=pl.ANY),
            scratch_shapes=[pltpu.VMEM((COMBINE_SLOTS, rows, d), F32), pltpu.SemaphoreType.DMA((COMBINE_SLOTS,)),
                            pltpu.SemaphoreType.DMA((COMBINE_SLOTS,))],
        ),
        out_shape=jax.ShapeDtypeStruct(x.shape, x.dtype),
        input_output_aliases={2: 0},
        compiler_params=_cparams("arbitrary"),
        name="combine",
    )(idx, ye, x)


def _rel_bucket(rel):
    nb = REL_BUCKETS // 2
    max_exact = nb // 2
    ret = jnp.where(rel > 0, nb, 0)
    n = jnp.abs(rel)
    nf = jnp.maximum(n, 1).astype(F32)
    large = max_exact + (jnp.log(nf / max_exact) / math.log(REL_MAX_DIST / max_exact)
                         * (nb - max_exact)).astype(jnp.int32)
    large = jnp.minimum(large, nb - 1)
    return ret + jnp.where(n < max_exact, n, large)


def _bucket_tiles(t):
    r = np.arange(t)[:, None]
    c = np.arange(t)[None, :]
    rel = np.stack([dt * t + (c - r) for dt in (-2, -1, 0, 1, 2)]).astype(np.int32)
    return _rel_bucket(jnp.asarray(rel)).astype(jnp.int32)


def _axial_rope_tables(s):
    rows = s // GRID_W
    row = jnp.repeat(jnp.arange(rows), GRID_W).astype(F32)
    col = jnp.tile(jnp.arange(GRID_W), rows).astype(F32)
    half = HEAD_DIM // 2
    inv = 1.0 / (ROPE_THETA ** (jnp.arange(0, half, 2, dtype=F32) / half))
    ang_r = row[:, None] * inv
    ang_c = col[:, None] * inv
    ang = jnp.concatenate([ang_r, ang_r, ang_c, ang_c], axis=-1)
    sign = np.where((np.arange(HEAD_DIM) % half) < half // 2, -1.0, 1.0).astype(np.float32)
    return jnp.cos(ang), jnp.sin(ang) * sign


def _layer(x, mem, rel_table, g_mix, w_in, g_qa, g_ka, lam_vecs, g_subln, g_qb, g_kb, w_o, g_cross,
           g_mem, wq_x, wk_x, wv_x, g_qx, g_kx, wo_x, g_ffn, w_router, w_gate, w_up, w_down, lam_init):
    b, s, dm = x.shape
    n = b * s
    n_mem = mem.shape[1]
    x2d = x.reshape(n, dm)
    row = lambda v: v.reshape(1, -1)

    h = rmsnorm_cast(x2d, g_mix)
    proj = matmul([h], w_in).reshape(b, s, -1)
    t_a = min(256, s)
    assert s % t_a == 0 and t_a >= REL_MAX_DIST, "bias tiles need saturated buckets two tiles away"
    bounded_a = _bounded_flag(_score_bound(g_qa, g_ka, jnp.max(jnp.abs(rel_table))))
    oa = diff_attention(proj, bounded_a, rel_table, _bucket_tiles(t_a), row(g_qa), row(g_ka), row(g_subln),
                        lam_vecs, lam_init, t_a)
    cos, sin_signed = _axial_rope_tables(s)
    ob = gqa_attention(proj, _bounded_flag(_score_bound(g_qb, g_kb)), cos, sin_signed, row(g_qb), row(g_kb))
    x1 = matmul([oa.reshape(n, -1), ob.reshape(n, -1)], w_o, res=x2d)

    m = rmsnorm_cast(mem.reshape(b * n_mem, dm), g_mem)
    kx = matmul([m], wk_x).reshape(b, n_mem, -1)
    vx = matmul([m], wv_x).reshape(b, n_mem, -1)
    x2, f_packed, logits = cross_ffn_block(x1, kx, vx, g_cross, wq_x.astype(BF16), row(g_qx), row(g_kx),
                                           wo_x.astype(BF16), g_ffn, w_router)

    e = w_router.shape[1]
    cap = EC_FACTOR * n // e
    idx, gates = route(logits.T, cap)
    idx_flat = idx.reshape(e * cap)
    hid = expert_gate_up(f_packed, idx_flat, e, w_gate, w_up)
    ye = expert_down(hid, w_down, gates)
    out = combine(x2, ye.reshape(e * cap, dm), idx_flat, rows=min(256, cap), rows_per_expert=cap)
    return out.reshape(b, s, dm)


def kernel(x_prompt, x_sample, mem_prompt, mem_sample, rel_table, g_mix, w_in, g_qa, g_ka, lam_q1, lam_k1, lam_q2, lam_k2, g_subln, g_qb, g_kb, w_o, g_cross, g_mem, wq_x, wk_x, wv_x, g_qx, g_kx, wo_x, g_ffn, w_router, w_gate, w_up, w_down):
    outs = []
    for x, mem in ((x_prompt, mem_prompt), (x_sample, mem_sample)):
        for l in range(g_mix.shape[0]):
            lam_init = 0.8 - 0.6 * math.exp(-0.3 * l)
            lam_vecs = jnp.stack([lam_q1[l], lam_k1[l], lam_q2[l], lam_k2[l]])
            x = _layer(x, mem, rel_table, g_mix[l], w_in[l], g_qa[l], g_ka[l], lam_vecs, g_subln[l], g_qb[l],
                       g_kb[l], w_o[l], g_cross[l], g_mem[l], wq_x[l], wk_x[l], wv_x[l], g_qx[l], g_kx[l],
                       wo_x[l], g_ffn[l], w_router[l], w_gate[l], w_up[l], w_down[l], lam_init)
        outs.append(x)
    return tuple(outs)
```

```python
import functools
import math

import numpy as np
import jax
import jax.numpy as jnp
from jax import lax
from jax.experimental import pallas as pl
from jax.experimental.pallas import tpu as pltpu

F32 = jnp.float32
BF16 = jnp.bfloat16

HEAD_DIM = 128
N_HEADS_A = 8
N_HEADS_B = 16
N_KV_B = 4
N_HEADS_X = 4
EC_FACTOR = 2
GRID_W = 64
REL_BUCKETS = 32
REL_MAX_DIST = 128
ROPE_THETA = 10000.0
EPS = 1e-6
ATTN_SCALE = HEAD_DIM ** -0.5
LOG2E = math.log2(math.e)

V7X_VMEM_BYTES = 64 * 1024 * 1024
VMEM_LIMIT = V7X_VMEM_BYTES - 8 * 1024 * 1024
LANES = 128
SUBLANES = 8

SAFE_SCORE_BOUND = 30.0
BF16_NORM_SLACK = (1.0 + 2.0 ** -8) ** 2
KEY_TILES_PER_TRIP = 8
DIFF_QUERY_TILES_PER_STEP = 2

NT_DIMS = (((1,), (1,)), ((), ()))


def _cparams(*sem):
    return pltpu.CompilerParams(dimension_semantics=sem, vmem_limit_bytes=VMEM_LIMIT)


def _rms(x, g):
    x = x.astype(F32)
    return x * lax.rsqrt(jnp.mean(x * x, axis=-1, keepdims=True) + EPS) * g


def _smem_spec():
    return pl.BlockSpec(memory_space=pltpu.SMEM)


def _rmsnorm_body(x_ref, g_ref, o_ref):
    o_ref[...] = _rms(x_ref[...], g_ref[...]).astype(o_ref.dtype)


def rmsnorm_cast(x, g, tm=256):
    m, d = x.shape
    tm = min(tm, m)
    return pl.pallas_call(
        _rmsnorm_body,
        grid=(m // tm,),
        in_specs=[pl.BlockSpec((tm, d), lambda i: (i, 0)), pl.BlockSpec((1, d), lambda i: (0, 0))],
        out_specs=pl.BlockSpec((tm, d), lambda i: (i, 0)),
        out_shape=jax.ShapeDtypeStruct((m, d), BF16),
        compiler_params=_cparams("arbitrary"),
        name="rmsnorm_cast",
    )(x, g.reshape(1, d))


def _mm_body(*refs, n_a, has_res):
    a_refs, w_ref = refs[:n_a], refs[n_a]
    r_ref = refs[n_a + 1] if has_res else None
    o_ref, wb_ref = refs[-2:]

    @pl.when(pl.program_id(1) == 0)
    def _():
        wb_ref[...] = w_ref[...].astype(BF16)

    acc, k0 = None, 0
    for a_ref in a_refs:
        k = a_ref.shape[1]
        part = jnp.dot(a_ref[...], wb_ref[k0:k0 + k, :], preferred_element_type=F32)
        acc = part if acc is None else acc + part
        k0 += k
    if has_res:
        acc = acc + r_ref[...]
    o_ref[...] = acc.astype(o_ref.dtype)


def matmul(a_list, w, res=None, out_dtype=F32, tm=1024, tn=1024):
    m = a_list[0].shape[0]
    k, n = w.shape
    assert sum(a.shape[1] for a in a_list) == k
    tm, tn = min(tm, m), min(tn, n)
    in_specs = [pl.BlockSpec((tm, a.shape[1]), lambda j, i: (i, 0)) for a in a_list]
    in_specs.append(pl.BlockSpec((k, tn), lambda j, i: (0, j), pipeline_mode=pl.Buffered(1)))
    args = list(a_list) + [w]
    if res is not None:
        in_specs.append(pl.BlockSpec((tm, tn), lambda j, i: (i, j)))
        args.append(res)
    return pl.pallas_call(
        functools.partial(_mm_body, n_a=len(a_list), has_res=res is not None),
        grid=(n // tn, m // tm),
        in_specs=in_specs,
        out_specs=pl.BlockSpec((tm, tn), lambda j, i: (i, j)),
        out_shape=jax.ShapeDtypeStruct((m, n), out_dtype),
        scratch_shapes=[pltpu.VMEM((k, tn), BF16)],
        compiler_params=_cparams("arbitrary", "arbitrary"),
        name="matmul",
    )(*args)


def _online_softmax_step(s, v, m_ref, l_ref, acc_ref):
    m_old = m_ref[...]
    m_new = jnp.maximum(m_old, jnp.max(s, axis=-1, keepdims=True))
    alpha = jnp.exp2(m_old - m_new)
    p = jnp.exp2(s - m_new)
    l_ref[...] = alpha * l_ref[...] + jnp.sum(p, axis=-1, keepdims=True)
    acc_ref[...] = alpha * acc_ref[...] + jnp.dot(p.astype(BF16), v, preferred_element_type=F32)
    m_ref[...] = m_new


def _reset_online(m_ref, l_ref, acc_ref):
    m_ref[...] = jnp.full(m_ref.shape, -jnp.inf, F32)
    l_ref[...] = jnp.zeros(l_ref.shape, F32)
    acc_ref[...] = jnp.zeros(acc_ref.shape, F32)


def _rope(x, cos, sin_signed):
    lane = lax.broadcasted_iota(jnp.int32, x.shape, 1)
    first = (lane % (HEAD_DIM // 2)) < (HEAD_DIM // 4)
    xr = jnp.where(first, pltpu.roll(x, HEAD_DIM - HEAD_DIM // 4, 1), pltpu.roll(x, HEAD_DIM // 4, 1))
    return x * cos + xr * sin_signed


def _score_bound(g_q, g_k, extra=0.0):
    return HEAD_DIM * ATTN_SCALE * BF16_NORM_SLACK * jnp.max(jnp.abs(g_q)) * jnp.max(jnp.abs(g_k)) + extra


def _bounded_flag(bound):
    return (bound <= SAFE_SCORE_BOUND).astype(jnp.int32).reshape(1)


def _diff_attn_body(flag_ref, tab_ref, q_ref, k_ref, v_ref, bkt_ref, gq_ref, gk_ref, gs_ref, lam_ref, o_ref,
                    kt_ref, vb_ref, bias_ref, q_scr, p_ref, lsum_ref, m_ref, l_ref, acc_ref,
                    *, t, sub, n_kt, group, lam_init):
    hi = pl.program_id(1)
    qi = pl.program_id(2)
    d = HEAD_DIM

    @pl.when(qi == 0)
    def _():
        for j in range(2):
            kj = k_ref[0, :, j * d:(j + 1) * d]
            kt_ref[j] = _rms(kj, gk_ref[...]).T.astype(BF16)
        vb_ref[...] = v_ref[0].astype(BF16)

        def build(ti, c):
            bk = bkt_ref[ti]
            acc = jnp.zeros((t, t), F32)
            for b in range(REL_BUCKETS):
                acc = jnp.where(bk == b, tab_ref[b, hi], acc)
            bias_ref[ti] = acc * LOG2E
            return c
        lax.fori_loop(0, 5, build, 0)

    for j in range(2):
        qj = q_ref[0, :, j * d:(j + 1) * d]
        q_scr[j] = (_rms(qj, gq_ref[...]) * (ATTN_SCALE * LOG2E)).astype(BF16)

    def finish(o1, o2):
        lam_v = lam_ref[...]
        lam = (jnp.exp(jnp.sum(lam_v[0:1] * lam_v[1:2], axis=-1, keepdims=True))
               - jnp.exp(jnp.sum(lam_v[2:3] * lam_v[3:4], axis=-1, keepdims=True)) + lam_init)
        o = o1 - lam * o2
        o_ref[0] = (_rms(o, gs_ref[...]) * (1.0 - lam_init)).astype(o_ref.dtype)

    def tile_scores(kt, j):
        k0 = pl.multiple_of(kt * t, t)
        s = jnp.dot(q_scr[j], kt_ref[j, :, pl.ds(k0, t)], preferred_element_type=F32)
        rows = [s[u * t:(u + 1) * t] + bias_ref[jnp.clip(kt - (qi * sub + u), -2, 2) + 2] for u in range(sub)]
        return rows[0] if sub == 1 else jnp.concatenate(rows, axis=0)

    @pl.when(flag_ref[0] == 1)
    def _():
        lsum_ref[...] = jnp.zeros(lsum_ref.shape, F32)

        def step(kg, c):
            parts = [None, None]
            for u in range(group):
                kt = kg * group + u
                k0 = pl.multiple_of(kt * t, t)
                for j in range(2):
                    p = jnp.exp2(tile_scores(kt, j))
                    p_ref[j, :, pl.ds(k0, t)] = p.astype(BF16)
                    for c0 in range(0, t, LANES):
                        pc = p[:, c0:c0 + LANES]
                        parts[j] = pc if parts[j] is None else parts[j] + pc
            for j in range(2):
                lsum_ref[j] += parts[j]
            return c
        lax.fori_loop(0, n_kt // group, step, 0)
        outs = []
        for j in range(2):
            l = jnp.sum(lsum_ref[j], axis=-1, keepdims=True)
            outs.append(jnp.dot(p_ref[j], vb_ref[...], preferred_element_type=F32) / l)
        finish(outs[0], outs[1])

    @pl.when(flag_ref[0] != 1)
    def _():
        _reset_online(m_ref, l_ref, acc_ref)

        def step(kt, c):
            k0 = pl.multiple_of(kt * t, t)
            vt = vb_ref[pl.ds(k0, t), :]
            for j in range(2):
                _online_softmax_step(tile_scores(kt, j), vt, m_ref.at[j], l_ref.at[j], acc_ref.at[j])
            return c
        lax.fori_loop(0, n_kt, step, 0)
        finish(acc_ref[0] / l_ref[0], acc_ref[1] / l_ref[1])


def diff_attention(proj, bounded, rel_table, bucket_tiles, g_qa, g_ka, g_subln, lam_vecs, lam_init, t):
    b, s, _ = proj.shape
    h = N_HEADS_A
    w = 2 * HEAD_DIM
    n_kt = s // t
    group = math.gcd(n_kt, KEY_TILES_PER_TRIP)
    sub = math.gcd(n_kt, DIFF_QUERY_TILES_PER_STEP)
    tq = sub * t
    return pl.pallas_call(
        functools.partial(_diff_attn_body, t=t, sub=sub, n_kt=n_kt, group=group, lam_init=lam_init),
        grid=(b, h, s // tq),
        in_specs=[
            _smem_spec(),
            _smem_spec(),
            pl.BlockSpec((1, tq, w), lambda bi, hi, qi: (bi, qi, hi)),
            pl.BlockSpec((1, s, w), lambda bi, hi, qi: (bi, 0, h + hi)),
            pl.BlockSpec((1, s, w), lambda bi, hi, qi: (bi, 0, 2 * h + hi)),
            pl.BlockSpec((5, t, t), lambda bi, hi, qi: (0, 0, 0)),
            pl.BlockSpec((1, HEAD_DIM), lambda bi, hi, qi: (0, 0)),
            pl.BlockSpec((1, HEAD_DIM), lambda bi, hi, qi: (0, 0)),
            pl.BlockSpec((1, w), lambda bi, hi, qi: (0, 0)),
            pl.BlockSpec((4, HEAD_DIM), lambda bi, hi, qi: (0, 0)),
        ],
        out_specs=pl.BlockSpec((1, tq, w), lambda bi, hi, qi: (bi, qi, hi)),
        out_shape=jax.ShapeDtypeStruct((b, s, h * w), BF16),
        scratch_shapes=[
            pltpu.VMEM((2, HEAD_DIM, s), BF16), pltpu.VMEM((s, w), BF16), pltpu.VMEM((5, t, t), F32),
            pltpu.VMEM((2, tq, HEAD_DIM), BF16), pltpu.VMEM((2, tq, s), BF16), pltpu.VMEM((2, tq, LANES), F32),
            pltpu.VMEM((2, tq, 1), F32), pltpu.VMEM((2, tq, 1), F32), pltpu.VMEM((2, tq, w), F32),
        ],
        compiler_params=_cparams("arbitrary", "arbitrary", "arbitrary"),
        name="diff_attention",
    )(bounded, rel_table, proj, proj, proj, bucket_tiles, g_qa, g_ka, g_subln, lam_vecs)


def _gqa_body(flag_ref, q_ref, k_ref, v_ref, cq_ref, sq_ref, ck_ref, sk_ref, gq_ref, gk_ref, o_ref,
              kt_ref, va_ref, q_scr, p_ref, m_ref, l_ref, acc_ref, *, tq, tk, n_kt, group, rep):
    qi = pl.program_id(2)
    d = HEAD_DIM

    @pl.when(qi == 0)
    def _():
        kn = _rope(_rms(k_ref[0], gk_ref[...]), ck_ref[...], sk_ref[...])
        kt_ref[...] = kn.T.astype(BF16)
        va_ref[:, 0:d] = v_ref[0].astype(BF16)
        lane = lax.broadcasted_iota(jnp.int32, (va_ref.shape[0], d), 1)
        va_ref[:, d:2 * d] = jnp.where(lane == 0, 1.0, 0.0).astype(BF16)

    cq, sq = cq_ref[...], sq_ref[...]
    for r in range(rep):
        qr = _rope(_rms(q_ref[0, :, r * d:(r + 1) * d], gq_ref[...]), cq, sq)
        q_scr[r * tq:(r + 1) * tq, :] = (qr * (ATTN_SCALE * LOG2E)).astype(BF16)

    def tile_scores(kt):
        k0 = pl.multiple_of(kt * tk, tk)
        return jnp.dot(q_scr[...], kt_ref[:, pl.ds(k0, tk)], preferred_element_type=F32)

    def write(o):
        for r in range(rep):
            o_ref[0, :, r * d:(r + 1) * d] = o[r * tq:(r + 1) * tq].astype(o_ref.dtype)

    @pl.when(flag_ref[0] == 1)
    def _():
        def step(kg, c):
            for u in range(group):
                kt = kg * group + u
                k0 = pl.multiple_of(kt * tk, tk)
                p_ref[:, pl.ds(k0, tk)] = jnp.exp2(tile_scores(kt)).astype(BF16)
            return c
        lax.fori_loop(0, n_kt // group, step, 0)
        o = jnp.dot(p_ref[...], va_ref[...], preferred_element_type=F32)
        write(o[:, 0:d] / o[:, d:d + 1])

    @pl.when(flag_ref[0] != 1)
    def _():
        _reset_online(m_ref, l_ref, acc_ref)

        def step(kt, c):
            k0 = pl.multiple_of(kt * tk, tk)
            _online_softmax_step(tile_scores(kt), va_ref[pl.ds(k0, tk), 0:d], m_ref, l_ref, acc_ref)
            return c
        lax.fori_loop(0, n_kt, step, 0)
        write(acc_ref[...] / l_ref[...])


def gqa_attention(proj, bounded, cos, sin_signed, g_qb, g_kb, tq=256, tk=256):
    b, s, _ = proj.shape
    g = N_KV_B
    rep = N_HEADS_B // N_KV_B
    d = HEAD_DIM
    qw = rep * d
    tq, tk = min(tq, s), min(tk, s)
    a_w = N_HEADS_A * 2 * d
    q_blk0 = 3 * a_w // qw
    k_blk0 = (3 * a_w + N_HEADS_B * d) // d
    v_blk0 = k_blk0 + g
    rows = rep * tq
    return pl.pallas_call(
        functools.partial(_gqa_body, tq=tq, tk=tk, n_kt=s // tk, group=math.gcd(s // tk, KEY_TILES_PER_TRIP),
                          rep=rep),
        grid=(b, g, s // tq),
        in_specs=[
            _smem_spec(),
            pl.BlockSpec((1, tq, qw), lambda bi, gi, qi: (bi, qi, q_blk0 + gi)),
            pl.BlockSpec((1, s, d), lambda bi, gi, qi: (bi, 0, k_blk0 + gi)),
            pl.BlockSpec((1, s, d), lambda bi, gi, qi: (bi, 0, v_blk0 + gi)),
            pl.BlockSpec((tq, d), lambda bi, gi, qi: (qi, 0)),
            pl.BlockSpec((tq, d), lambda bi, gi, qi: (qi, 0)),
            pl.BlockSpec((s, d), lambda bi, gi, qi: (0, 0)),
            pl.BlockSpec((s, d), lambda bi, gi, qi: (0, 0)),
            pl.BlockSpec((1, d), lambda bi, gi, qi: (0, 0)),
            pl.BlockSpec((1, d), lambda bi, gi, qi: (0, 0)),
        ],
        out_specs=pl.BlockSpec((1, tq, qw), lambda bi, gi, qi: (bi, qi, gi)),
        out_shape=jax.ShapeDtypeStruct((b, s, g * qw), BF16),
        scratch_shapes=[
            pltpu.VMEM((d, s), BF16), pltpu.VMEM((s, 2 * d), BF16),
            pltpu.VMEM((rows, d), BF16), pltpu.VMEM((rows, s), BF16),
            pltpu.VMEM((rows, 1), F32), pltpu.VMEM((rows, 1), F32), pltpu.VMEM((rows, d), F32),
        ],
        compiler_params=_cparams("arbitrary", "arbitrary", "arbitrary"),
        name="gqa_attention",
    )(bounded, proj, proj, proj, cos, sin_signed, cos, sin_signed, g_qb, g_kb)


def _cross_heads(qx, k_ref, v_ref, gq, gk):
    d = HEAD_DIM
    outs = []
    for hh in range(N_HEADS_X):
        sl = slice(hh * d, (hh + 1) * d)
        qn = (_rms(qx[:, sl], gq) * ATTN_SCALE).astype(BF16)
        kn = _rms(k_ref[0, :, sl], gk).astype(BF16)
        s = lax.dot_general(qn, kn, NT_DIMS, preferred_element_type=F32)
        p = jnp.exp(s - jnp.max(s, axis=-1, keepdims=True))
        o = jnp.dot(p.astype(BF16), v_ref[0, :, sl].astype(BF16), preferred_element_type=F32)
        outs.append((o / jnp.sum(p, axis=-1, keepdims=True)).astype(BF16))
    return jnp.concatenate(outs, axis=1)


def _split_bf16(x):
    hi = x.astype(BF16)
    lo = (x - hi.astype(F32)).astype(BF16)
    return hi, lo


def _cross_ffn_body(x_ref, k_ref, v_ref, gc_ref, wq_ref, gq_ref, gk_ref, wo_ref, gf_ref, wr_ref,
                    x2_ref, fp_ref, lg_ref):
    x = x_ref[...]
    c = _rms(x, gc_ref[...]).astype(BF16)
    qx = jnp.dot(c, wq_ref[...], preferred_element_type=F32)
    ox = _cross_heads(qx, k_ref, v_ref, gq_ref[...], gk_ref[...])
    x2 = x + jnp.dot(ox, wo_ref[...], preferred_element_type=F32)
    x2_ref[...] = x2
    f = _rms(x2, gf_ref[...])
    half = f.shape[1] // 2
    fr = f.astype(BF16).astype(F32)
    lo_bits = pltpu.bitcast(fr[:, :half], jnp.uint32) >> 16
    hi_bits = pltpu.bitcast(fr[:, half:], jnp.uint32) & jnp.uint32(0xFFFF0000)
    fp_ref[...] = lo_bits | hi_bits
    f_hi, f_lo = _split_bf16(f)
    w_hi, w_lo = _split_bf16(wr_ref[...])
    lg_ref[...] = (jnp.dot(f_hi, w_hi, preferred_element_type=F32)
                   + jnp.dot(f_hi, w_lo, preferred_element_type=F32)
                   + jnp.dot(f_lo, w_hi, preferred_element_type=F32))


def cross_ffn_block(x, kx, vx, g_cross, wq, g_qx, g_kx, wo, g_ffn, w_router, tm=256):
    n, d = x.shape
    b, n_mem, xw = kx.shape
    e = w_router.shape[1]
    tm = min(tm, n // b)
    tiles_per_seq = n // b // tm
    row = lambda i: (i, 0)
    fixed = lambda i: (0, 0)
    mem = lambda i: (i // tiles_per_seq, 0, 0)
    return pl.pallas_call(
        _cross_ffn_body,
        grid=(n // tm,),
        in_specs=[
            pl.BlockSpec((tm, d), row),
            pl.BlockSpec((1, n_mem, xw), mem),
            pl.BlockSpec((1, n_mem, xw), mem),
            pl.BlockSpec((1, d), fixed),
            pl.BlockSpec((d, xw), fixed),
            pl.BlockSpec((1, HEAD_DIM), fixed),
            pl.BlockSpec((1, HEAD_DIM), fixed),
            pl.BlockSpec((xw, d), fixed),
            pl.BlockSpec((1, d), fixed),
            pl.BlockSpec((d, e), fixed),
        ],
        out_specs=[pl.BlockSpec((tm, d), row), pl.BlockSpec((tm, d // 2), row), pl.BlockSpec((tm, e), row)],
        out_shape=[jax.ShapeDtypeStruct((n, d), F32), jax.ShapeDtypeStruct((n, d // 2), jnp.uint32),
                   jax.ShapeDtypeStruct((n, e), F32)],
        compiler_params=_cparams("arbitrary"),
        name="cross_ffn_block",
    )(x, kx, vx, g_cross.reshape(1, d), wq, g_qx, g_kx, wo, g_ffn.reshape(1, d), w_router)


def _unpack_rows(p_ref, o_ref, rows=256):
    m, half = p_ref.shape
    rows = min(rows, m)
    for r0 in range(0, m, rows):
        p = p_ref[r0:r0 + rows, :]
        o_ref[r0:r0 + rows, :half] = pltpu.bitcast(p << 16, F32).astype(BF16)
        o_ref[r0:r0 + rows, half:] = pltpu.bitcast(p & jnp.uint32(0xFFFF0000), F32).astype(BF16)


def _select_top_cap(lg_ref, aff_ref, sel_ref, cs_ref, *, n, cap):
    e = lg_ref.shape[0]
    lg = lg_ref[...]
    ex = jnp.exp(lg - jnp.max(lg, axis=0, keepdims=True))
    aff = ex / jnp.sum(ex, axis=0, keepdims=True)
    aff_ref[...] = aff
    bits = pltpu.bitcast(aff, jnp.int32)

    def search(i, prefix):
        cand = prefix | (jnp.int32(1) << (30 - i))
        cnt = jnp.sum(jnp.where(bits >= cand, 1.0, 0.0), axis=1, keepdims=True)
        return jnp.where(cnt >= cap, cand, prefix)

    thr = lax.fori_loop(0, 31, search, jnp.zeros((e, 1), jnp.int32))
    gt = bits > thr
    need = cap - jnp.sum(jnp.where(gt, 1.0, 0.0), axis=1, keepdims=True)

    tri = (lax.broadcasted_iota(jnp.int32, (LANES, LANES), 0)
           <= lax.broadcasted_iota(jnp.int32, (LANES, LANES), 1)).astype(BF16)

    def cumsum_lanes(ref):
        def blk(j, carry):
            c0 = pl.multiple_of(j * LANES, LANES)
            w = jnp.dot(ref[:, pl.ds(c0, LANES)].astype(BF16), tri, preferred_element_type=F32) + carry
            ref[:, pl.ds(c0, LANES)] = w
            return w[:, LANES - 1:LANES]
        lax.fori_loop(0, n // LANES, blk, jnp.zeros((e, 1), F32))

    eq = bits == thr
    cs_ref[...] = jnp.where(eq, 1.0, 0.0)
    cumsum_lanes(cs_ref)
    sel = gt | (eq & (cs_ref[...] <= need))
    sel_ref[...] = jnp.where(sel, 1.0, 0.0)
    cs_ref[...] = sel_ref[...]
    cumsum_lanes(cs_ref)
    cs_ref[...] = cs_ref[...] * sel_ref[...]


def _route_body(lg_ref, idx_ref, gate_ref, aff_ref, sel_ref, cs_ref, aff_rows, cs_rows, *, n, cap, chunk):
    ei = pl.program_id(0)

    @pl.when(ei == 0)
    def _():
        _select_top_cap(lg_ref, aff_ref, sel_ref, cs_ref, n=n, cap=cap)
        for k in range(lg_ref.shape[0]):
            aff_rows[k] = aff_ref[k:k + 1, :]
            cs_rows[k] = cs_ref[k:k + 1, :]

    tok = lax.broadcasted_iota(jnp.int32, (1, n), 1).astype(F32)
    lane = lax.broadcasted_iota(jnp.int32, (chunk, LANES), 1).astype(F32)
    sub = lax.broadcasted_iota(jnp.int32, (chunk, 1), 0).astype(F32)

    def per_chunk(ci, carry):
        s0 = pl.multiple_of(ci * chunk, chunk)
        lo = s0.astype(F32)
        row = cs_rows[ei]
        inside = jnp.abs(row - (lo + 0.5 * (chunk + 1))) < 0.5 * chunk
        t_first = jnp.min(jnp.where(inside, tok, float(n)), axis=1, keepdims=True)[0, 0]
        t_last = jnp.max(jnp.where(inside, tok, -1.0), axis=1, keepdims=True)[0, 0]
        j_lo = t_first.astype(jnp.int32) // LANES
        j_hi = t_last.astype(jnp.int32) // LANES + 1
        want = sub + (lo + 1.0)

        def per_blk(j, accs):
            acc_i, acc_g = accs
            c0 = pl.multiple_of(j * LANES, LANES)
            hit = cs_rows[ei, :, pl.ds(c0, LANES)] == want
            acc_i = acc_i + jnp.where(hit, lane + c0.astype(F32), 0.0)
            acc_g = acc_g + jnp.where(hit, aff_rows[ei, :, pl.ds(c0, LANES)], 0.0)
            return acc_i, acc_g

        z = jnp.zeros((chunk, LANES), F32)
        acc_i, acc_g = lax.fori_loop(j_lo, j_hi, per_blk, (z, z))
        idx_ref[0, pl.ds(s0, chunk), :] = jnp.sum(acc_i, axis=1, keepdims=True).astype(jnp.int32)
        gate_ref[0, pl.ds(s0, chunk), :] = jnp.sum(acc_g, axis=1, keepdims=True)
        return carry

    lax.fori_loop(0, cap // chunk, per_chunk, 0)


def route(logits_t, cap):
    e, n = logits_t.shape
    chunk = min(LANES, cap)
    return pl.pallas_call(
        functools.partial(_route_body, n=n, cap=cap, chunk=chunk),
        grid=(e,),
        in_specs=[pl.BlockSpec((e, n), lambda i: (0, 0))],
        out_specs=[pl.BlockSpec((1, cap, 1), lambda i: (i, 0, 0)), pl.BlockSpec((1, cap, 1), lambda i: (i, 0, 0))],
        out_shape=[jax.ShapeDtypeStruct((e, cap, 1), jnp.int32), jax.ShapeDtypeStruct((e, cap, 1), F32)],
        scratch_shapes=[pltpu.VMEM((e, n), F32), pltpu.VMEM((e, n), F32), pltpu.VMEM((e, n), F32),
                        pltpu.VMEM((e, 1, n), F32), pltpu.VMEM((e, 1, n), F32)],
        compiler_params=_cparams("arbitrary"),
        name="route",
    )(logits_t)


def _row_copy(src_ref, dst_ref, src_row, dst_row, sem):
    return pltpu.make_async_copy(src_ref.at[pl.ds(src_row, 1)], dst_ref.at[pl.ds(dst_row, 1)], sem)


def _wait_rows(src_ref, dst_ref, n, sem):
    pltpu.make_async_copy(src_ref.at[pl.ds(0, n)], dst_ref.at[pl.ds(0, n)], sem).wait()


ROW_DMA_UNROLL = 8


def _gate_up_body(idx_ref, f_ref, wg_ref, wu_ref, o_ref, xg_ref, xb_ref, sems, *, tm, n_f, n_tiles):
    fi = pl.program_id(2)
    tile = pl.program_id(0) * pl.num_programs(1) + pl.program_id(1)
    slot = tile % 2
    share = tm // n_f

    def start_row(tile_id, slot_id, r):
        _row_copy(f_ref, xg_ref.at[slot_id], idx_ref[tile_id * tm + r], r, sems.at[slot_id]).start()

    @pl.when((tile == 0) & (fi == 0))
    def _():
        def issue(r, c):
            start_row(0, 0, r)
            return c
        lax.fori_loop(0, tm, issue, 0, unroll=ROW_DMA_UNROLL)

    @pl.when(fi == 0)
    def _():
        _wait_rows(f_ref, xg_ref.at[slot], tm, sems.at[slot])
        _unpack_rows(xg_ref.at[slot], xb_ref)

    nxt = (tile + 1) % n_tiles
    for r in range(share):
        start_row(nxt, 1 - slot, fi * share + r)

    x = xb_ref[...]
    g = jnp.dot(x, wg_ref[0].astype(BF16), preferred_element_type=F32)
    u = jnp.dot(x, wu_ref[0].astype(BF16), preferred_element_type=F32)
    o_ref[0] = (g * (1.0 / (1.0 + jnp.exp(-g))) * u).astype(o_ref.dtype)

    @pl.when((tile == n_tiles - 1) & (fi == n_f - 1))
    def _():
        _wait_rows(f_ref, xg_ref.at[1 - slot], tm, sems.at[1 - slot])


def expert_gate_up(f_packed, idx, e, w_gate, w_up, tm=1024, tf=256):
    c = idx.shape[0] // e
    dh = f_packed.shape[1]
    d = 2 * dh
    f = w_gate.shape[2]
    tm, tf = min(tm, c), min(tf, f)
    n_f = f // tf
    assert c % tm == 0 and tm % n_f == 0
    return pl.pallas_call(
        functools.partial(_gate_up_body, tm=tm, n_f=n_f, n_tiles=e * (c // tm)),
        grid_spec=pltpu.PrefetchScalarGridSpec(
            num_scalar_prefetch=1,
            grid=(e, c // tm, n_f),
            in_specs=[
                pl.BlockSpec(memory_space=pl.ANY),
                pl.BlockSpec((1, d, tf), lambda ei, mi, fi, idx_ref: (ei, 0, fi)),
                pl.BlockSpec((1, d, tf), lambda ei, mi, fi, idx_ref: (ei, 0, fi)),
            ],
            out_specs=pl.BlockSpec((1, tm, tf), lambda ei, mi, fi, idx_ref: (ei, mi, fi)),
            scratch_shapes=[pltpu.VMEM((2, tm, dh), jnp.uint32), pltpu.VMEM((tm, d), BF16),
                            pltpu.SemaphoreType.DMA((2,))],
        ),
        out_shape=jax.ShapeDtypeStruct((e, c, f), BF16),
        compiler_params=_cparams("arbitrary", "arbitrary", "arbitrary"),
        name="expert_gate_up",
    )(idx, f_packed, w_gate, w_up)


def _down_body(h_ref, w_ref, gate_ref, o_ref):
    y = jnp.dot(h_ref[0], w_ref[0].astype(BF16), preferred_element_type=F32) * gate_ref[0]
    o_ref[0] = y.astype(o_ref.dtype)


def expert_down(hid, w_down, gates, tm=1024, tn=512):
    e, c, f = hid.shape
    d = w_down.shape[2]
    tm, tn = min(tm, c), min(tn, d)
    return pl.pallas_call(
        _down_body,
        grid=(e, c // tm, d // tn),
        in_specs=[
            pl.BlockSpec((1, tm, f), lambda ei, mi, ni: (ei, mi, 0)),
            pl.BlockSpec((1, f, tn), lambda ei, mi, ni: (ei, 0, ni)),
            pl.BlockSpec((1, tm, 1), lambda ei, mi, ni: (ei, mi, 0)),
        ],
        out_specs=pl.BlockSpec((1, tm, tn), lambda ei, mi, ni: (ei, mi, ni)),
        out_shape=jax.ShapeDtypeStruct((e, c, d), BF16),
        compiler_params=_cparams("arbitrary", "arbitrary", "arbitrary"),
        name="expert_down",
    )(hid, w_down, gates)


COMBINE_SLOTS = 3


def _combine_body(idx_ref, ye_ref, x_ref, o_ref, buf_ref, sem_in, sem_out, *, rows, tpe, n_tiles):
    del x_ref
    i = pl.program_id(0)
    slot = i % COMBINE_SLOTS
    first = (i % tpe) == 0
    last = (i % tpe) == tpe - 1

    def fetch(tile, s):
        def one(r, c):
            _row_copy(o_ref, buf_ref.at[s], idx_ref[tile * rows + r], r, sem_in.at[s]).start()
            return c
        lax.fori_loop(0, rows, one, 0, unroll=ROW_DMA_UNROLL)

    def put(tile, s):
        def one(r, c):
            _row_copy(buf_ref.at[s], o_ref, r, idx_ref[tile * rows + r], sem_out.at[s]).start()
            return c
        lax.fori_loop(0, rows, one, 0, unroll=ROW_DMA_UNROLL)

    def wait_put(s):
        _wait_rows(buf_ref.at[s], o_ref, rows, sem_out.at[s])

    @pl.when((i >= 2) & (((i - 1) % tpe) != 0))
    def _():
        wait_put((i - 2) % COMBINE_SLOTS)

    @pl.when(first)
    def _():
        @pl.when(i >= 1)
        def _():
            wait_put((i - 1) % COMBINE_SLOTS)
        fetch(i, slot)

    @pl.when(jnp.logical_not(last))
    def _():
        fetch(i + 1, (i + 1) % COMBINE_SLOTS)

    _wait_rows(o_ref, buf_ref.at[slot], rows, sem_in.at[slot])
    buf_ref[slot] = buf_ref[slot] + ye_ref[...].astype(F32)
    put(i, slot)

    @pl.when(i == n_tiles - 1)
    def _():
        if n_tiles >= 2 and (n_tiles - 1) % tpe != 0:
            wait_put((n_tiles - 2) % COMBINE_SLOTS)
        wait_put((n_tiles - 1) % COMBINE_SLOTS)


def combine(x, ye, idx, rows, rows_per_expert):
    r, d = ye.shape
    assert rows_per_expert % rows == 0
    return pl.pallas_call(
        functools.partial(_combine_body, rows=rows, tpe=rows_per_expert // rows, n_tiles=r // rows),
        grid_spec=pltpu.PrefetchScalarGridSpec(
            num_scalar_prefetch=1,
            grid=(r // rows,),
            in_specs=[pl.BlockSpec((rows, d), lambda i, idx_ref: (i, 0)), pl.BlockSpec(memory_space=pl.ANY)],
            out_specs=pl.BlockSpec(memory_space=pl.ANY),
            scratch_shapes=[pltpu.VMEM((COMBINE_SLOTS, rows, d), F32), pltpu.SemaphoreType.DMA((COMBINE_SLOTS,)),
                            pltpu.SemaphoreType.DMA((COMBINE_SLOTS,))],
        ),
        out_shape=jax.ShapeDtypeStruct(x.shape, x.dtype),
        input_output_aliases={2: 0},
        compiler_params=_cparams("arbitrary"),
        name="combine",
    )(idx, ye, x)


def _rel_bucket(rel):
    nb = REL_BUCKETS // 2
    max_exact = nb // 2
    ret = jnp.where(rel > 0, nb, 0)
    n = jnp.abs(rel)
    nf = jnp.maximum(n, 1).astype(F32)
    large = max_exact + (jnp.log(nf / max_exact) / math.log(REL_MAX_DIST / max_exact)
                         * (nb - max_exact)).astype(jnp.int32)
    large = jnp.minimum(large, nb - 1)
    return ret + jnp.where(n < max_exact, n, large)


def _bucket_tiles(t):
    r = np.arange(t)[:, None]
    c = np.arange(t)[None, :]
    rel = np.stack([dt * t + (c - r) for dt in (-2, -1, 0, 1, 2)]).astype(np.int32)
    return _rel_bucket(jnp.asarray(rel)).astype(jnp.int32)


def _axial_rope_tables(s):
    rows = s // GRID_W
    row = jnp.repeat(jnp.arange(rows), GRID_W).astype(F32)
    col = jnp.tile(jnp.arange(GRID_W), rows).astype(F32)
    half = HEAD_DIM // 2
    inv = 1.0 / (ROPE_THETA ** (jnp.arange(0, half, 2, dtype=F32) / half))
    ang_r = row[:, None] * inv
    ang_c = col[:, None] * inv
    ang = jnp.concatenate([ang_r, ang_r, ang_c, ang_c], axis=-1)
    sign = np.where((np.arange(HEAD_DIM) % half) < half // 2, -1.0, 1.0).astype(np.float32)
    return jnp.cos(ang), jnp.sin(ang) * sign


def _layer(x, mem, rel_table, g_mix, w_in, g_qa, g_ka, lam_vecs, g_subln, g_qb, g_kb, w_o, g_cross,
           g_mem, wq_x, wk_x, wv_x, g_qx, g_kx, wo_x, g_ffn, w_router, w_gate, w_up, w_down, lam_init):
    b, s, dm = x.shape
    n = b * s
    n_mem = mem.shape[1]
    x2d = x.reshape(n, dm)
    row = lambda v: v.reshape(1, -1)

    h = rmsnorm_cast(x2d, g_mix)
    proj = matmul([h], w_in, out_dtype=BF16).reshape(b, s, -1)
    t_a = min(256, s)
    assert s % t_a == 0 and t_a >= REL_MAX_DIST, "bias tiles need saturated buckets two tiles away"
    bounded_a = _bounded_flag(_score_bound(g_qa, g_ka, jnp.max(jnp.abs(rel_table))))
    oa = diff_attention(proj, bounded_a, rel_table, _bucket_tiles(t_a), row(g_qa), row(g_ka), row(g_subln),
                        lam_vecs, lam_init, t_a)
    cos, sin_signed = _axial_rope_tables(s)
    ob = gqa_attention(proj, _bounded_flag(_score_bound(g_qb, g_kb)), cos, sin_signed, row(g_qb), row(g_kb))
    x1 = matmul([oa.reshape(n, -1), ob.reshape(n, -1)], w_o, res=x2d, tm=512)

    m = rmsnorm_cast(mem.reshape(b * n_mem, dm), g_mem)
    kx = matmul([m], wk_x).reshape(b, n_mem, -1)
    vx = matmul([m], wv_x).reshape(b, n_mem, -1)
    x2, f_packed, logits = cross_ffn_block(x1, kx, vx, g_cross, wq_x.astype(BF16), row(g_qx), row(g_kx),
                                           wo_x.astype(BF16), g_ffn, w_router)

    e = w_router.shape[1]
    cap = EC_FACTOR * n // e
    idx, gates = route(logits.T, cap)
    idx_flat = idx.reshape(e * cap)
    hid = expert_gate_up(f_packed, idx_flat, e, w_gate, w_up)
    ye = expert_down(hid, w_down, gates)
    out = combine(x2, ye.reshape(e * cap, dm), idx_flat, rows=min(256, cap), rows_per_expert=cap)
    return out.reshape(b, s, dm)


def kernel(x_prompt, x_sample, mem_prompt, mem_sample, rel_table, g_mix, w_in, g_qa, g_ka, lam_q1, lam_k1, lam_q2, lam_k2, g_subln, g_qb, g_kb, w_o, g_cross, g_mem, wq_x, wk_x, wv_x, g_qx, g_kx, wo_x, g_ffn, w_router, w_gate, w_up, w_down):
    outs = []
    for x, mem in ((x_prompt, mem_prompt), (x_sample, mem_sample)):
        for l in range(g_mix.shape[0]):
            lam_init = 0.8 - 0.6 * math.exp(-0.3 * l)
            lam_vecs = jnp.stack([lam_q1[l], lam_k1[l], lam_q2[l], lam_k2[l]])
            x = _layer(x, mem, rel_table, g_mix[l], w_in[l], g_qa[l], g_ka[l], lam_vecs, g_subln[l], g_qb[l],
                       g_kb[l], w_o[l], g_cross[l], g_mem[l], wq_x[l], wk_x[l], wv_x[l], g_qx[l], g_kx[l],
                       wo_x[l], g_ffn[l], w_router[l], w_gate[l], w_up[l], w_down[l], lam_init)
        outs.append(x)
    return tuple(outs)
```

```python
import functools
import math

import numpy as np
import jax
import jax.numpy as jnp
from jax import lax
from jax.experimental import pallas as pl
from jax.experimental.pallas import tpu as pltpu

F32 = jnp.float32
BF16 = jnp.bfloat16

HEAD_DIM = 128
N_HEADS_A = 8
N_HEADS_B = 16
N_KV_B = 4
N_HEADS_X = 4
EC_FACTOR = 2
GRID_W = 64
REL_BUCKETS = 32
REL_MAX_DIST = 128
ROPE_THETA = 10000.0
EPS = 1e-6
ATTN_SCALE = HEAD_DIM ** -0.5
LOG2E = math.log2(math.e)

V7X_VMEM_BYTES = 64 * 1024 * 1024
VMEM_LIMIT = V7X_VMEM_BYTES - 8 * 1024 * 1024
LANES = 128
SUBLANES = 8

SAFE_SCORE_BOUND = 30.0
BF16_NORM_SLACK = (1.0 + 2.0 ** -8) ** 2
SCORE_MATMUL_KEYS = 1024
SCORE_MATMULS_PER_TRIP = 4
DIFF_QUERY_TILES_PER_STEP = 2

NT_DIMS = (((1,), (1,)), ((), ()))


def _cparams(*sem):
    return pltpu.CompilerParams(dimension_semantics=sem, vmem_limit_bytes=VMEM_LIMIT)


def _rms(x, g):
    x = x.astype(F32)
    return x * lax.rsqrt(jnp.mean(x * x, axis=-1, keepdims=True) + EPS) * g


def _smem_spec():
    return pl.BlockSpec(memory_space=pltpu.SMEM)


def _rmsnorm_body(x_ref, g_ref, o_ref):
    o_ref[...] = _rms(x_ref[...], g_ref[...]).astype(o_ref.dtype)


def rmsnorm_cast(x, g, tm=256):
    m, d = x.shape
    tm = min(tm, m)
    return pl.pallas_call(
        _rmsnorm_body,
        grid=(m // tm,),
        in_specs=[pl.BlockSpec((tm, d), lambda i: (i, 0)), pl.BlockSpec((1, d), lambda i: (0, 0))],
        out_specs=pl.BlockSpec((tm, d), lambda i: (i, 0)),
        out_shape=jax.ShapeDtypeStruct((m, d), BF16),
        compiler_params=_cparams("arbitrary"),
        name="rmsnorm_cast",
    )(x, g.reshape(1, d))


def _mm_body(*refs, n_a, has_res):
    a_refs, w_ref = refs[:n_a], refs[n_a]
    r_ref = refs[n_a + 1] if has_res else None
    o_ref, wb_ref = refs[-2:]

    @pl.when(pl.program_id(1) == 0)
    def _():
        wb_ref[...] = w_ref[...].astype(BF16)

    acc, k0 = None, 0
    for a_ref in a_refs:
        k = a_ref.shape[1]
        part = jnp.dot(a_ref[...], wb_ref[k0:k0 + k, :], preferred_element_type=F32)
        acc = part if acc is None else acc + part
        k0 += k
    if has_res:
        acc = acc + r_ref[...]
    o_ref[...] = acc.astype(o_ref.dtype)


def matmul(a_list, w, res=None, out_dtype=F32, tm=1024, tn=1024):
    m = a_list[0].shape[0]
    k, n = w.shape
    assert sum(a.shape[1] for a in a_list) == k
    tm, tn = min(tm, m), min(tn, n)
    in_specs = [pl.BlockSpec((tm, a.shape[1]), lambda j, i: (i, 0)) for a in a_list]
    in_specs.append(pl.BlockSpec((k, tn), lambda j, i: (0, j), pipeline_mode=pl.Buffered(1)))
    args = list(a_list) + [w]
    if res is not None:
        in_specs.append(pl.BlockSpec((tm, tn), lambda j, i: (i, j)))
        args.append(res)
    return pl.pallas_call(
        functools.partial(_mm_body, n_a=len(a_list), has_res=res is not None),
        grid=(n // tn, m // tm),
        in_specs=in_specs,
        out_specs=pl.BlockSpec((tm, tn), lambda j, i: (i, j)),
        out_shape=jax.ShapeDtypeStruct((m, n), out_dtype),
        scratch_shapes=[pltpu.VMEM((k, tn), BF16)],
        compiler_params=_cparams("arbitrary", "arbitrary"),
        name="matmul",
    )(*args)


def _online_softmax_step(s, v, m_ref, l_ref, acc_ref):
    m_old = m_ref[...]
    m_new = jnp.maximum(m_old, jnp.max(s, axis=-1, keepdims=True))
    alpha = jnp.exp2(m_old - m_new)
    p = jnp.exp2(s - m_new)
    l_ref[...] = alpha * l_ref[...] + jnp.sum(p, axis=-1, keepdims=True)
    acc_ref[...] = alpha * acc_ref[...] + jnp.dot(p.astype(BF16), v, preferred_element_type=F32)
    m_ref[...] = m_new


def _reset_online(m_ref, l_ref, acc_ref):
    m_ref[...] = jnp.full(m_ref.shape, -jnp.inf, F32)
    l_ref[...] = jnp.zeros(l_ref.shape, F32)
    acc_ref[...] = jnp.zeros(acc_ref.shape, F32)


def _rope(x, cos, sin_signed):
    lane = lax.broadcasted_iota(jnp.int32, x.shape, 1)
    first = (lane % (HEAD_DIM // 2)) < (HEAD_DIM // 4)
    xr = jnp.where(first, pltpu.roll(x, HEAD_DIM - HEAD_DIM // 4, 1), pltpu.roll(x, HEAD_DIM // 4, 1))
    return x * cos + xr * sin_signed


def _score_bound(g_q, g_k, extra=0.0):
    return HEAD_DIM * ATTN_SCALE * BF16_NORM_SLACK * jnp.max(jnp.abs(g_q)) * jnp.max(jnp.abs(g_k)) + extra


def _bounded_flag(bound):
    return (bound <= SAFE_SCORE_BOUND).astype(jnp.int32).reshape(1)


def _diff_attn_body(flag_ref, tab_ref, q_ref, k_ref, v_ref, bkt_ref, gq_ref, gk_ref, gs_ref, lam_ref, o_ref,
                    kt_ref, vb_ref, bias_ref, q_scr, p_ref, lsum_ref, m_ref, l_ref, acc_ref,
                    *, t, sub, cw, n_kt, group, lam_init):
    hi = pl.program_id(1)
    qi = pl.program_id(2)
    d = HEAD_DIM

    @pl.when(qi == 0)
    def _():
        for j in range(2):
            kj = k_ref[0, :, j * d:(j + 1) * d]
            kt_ref[j] = _rms(kj, gk_ref[...]).T.astype(BF16)
        vb_ref[...] = v_ref[0].astype(BF16)

        def build(ti, c):
            bk = bkt_ref[ti]
            acc = jnp.zeros((t, t), F32)
            for b in range(REL_BUCKETS):
                acc = jnp.where(bk == b, tab_ref[b, hi], acc)
            bias_ref[ti] = acc * LOG2E
            return c
        lax.fori_loop(0, 5, build, 0)

    for j in range(2):
        qj = q_ref[0, :, j * d:(j + 1) * d]
        q_scr[j] = (_rms(qj, gq_ref[...]) * (ATTN_SCALE * LOG2E)).astype(BF16)

    def finish(o1, o2):
        lam_v = lam_ref[...]
        lam = (jnp.exp(jnp.sum(lam_v[0:1] * lam_v[1:2], axis=-1, keepdims=True))
               - jnp.exp(jnp.sum(lam_v[2:3] * lam_v[3:4], axis=-1, keepdims=True)) + lam_init)
        o = o1 - lam * o2
        o_ref[0] = (_rms(o, gs_ref[...]) * (1.0 - lam_init)).astype(o_ref.dtype)

    kw = cw * t

    def block_scores(kb, j):
        k0 = pl.multiple_of(kb * kw, kw)
        s = jnp.dot(q_scr[j], kt_ref[j, :, pl.ds(k0, kw)], preferred_element_type=F32)
        rows = []
        for u in range(sub):
            cols = [s[u * t:(u + 1) * t, v * t:(v + 1) * t]
                    + bias_ref[jnp.clip(kb * cw + v - (qi * sub + u), -2, 2) + 2] for v in range(cw)]
            rows.append(cols[0] if cw == 1 else jnp.concatenate(cols, axis=1))
        return rows[0] if sub == 1 else jnp.concatenate(rows, axis=0)

    @pl.when(flag_ref[0] == 1)
    def _():
        lsum_ref[...] = jnp.zeros(lsum_ref.shape, F32)

        def step(kg, c):
            parts = [None, None]
            for u in range(group):
                kb = kg * group + u
                k0 = pl.multiple_of(kb * kw, kw)
                for j in range(2):
                    p = jnp.exp2(block_scores(kb, j))
                    p_ref[j, :, pl.ds(k0, kw)] = p.astype(BF16)
                    for c0 in range(0, kw, LANES):
                        pc = p[:, c0:c0 + LANES]
                        parts[j] = pc if parts[j] is None else parts[j] + pc
            for j in range(2):
                lsum_ref[j] += parts[j]
            return c
        lax.fori_loop(0, n_kt // (cw * group), step, 0)
        outs = []
        for j in range(2):
            l = jnp.sum(lsum_ref[j], axis=-1, keepdims=True)
            outs.append(jnp.dot(p_ref[j], vb_ref[...], preferred_element_type=F32) / l)
        finish(outs[0], outs[1])

    @pl.when(flag_ref[0] != 1)
    def _():
        _reset_online(m_ref, l_ref, acc_ref)

        def step(kb, c):
            k0 = pl.multiple_of(kb * kw, kw)
            vt = vb_ref[pl.ds(k0, kw), :]
            for j in range(2):
                _online_softmax_step(block_scores(kb, j), vt, m_ref.at[j], l_ref.at[j], acc_ref.at[j])
            return c
        lax.fori_loop(0, n_kt // cw, step, 0)
        finish(acc_ref[0] / l_ref[0], acc_ref[1] / l_ref[1])


def diff_attention(proj, bounded, rel_table, bucket_tiles, g_qa, g_ka, g_subln, lam_vecs, lam_init, t):
    b, s, _ = proj.shape
    h = N_HEADS_A
    w = 2 * HEAD_DIM
    n_kt = s // t
    cw = math.gcd(n_kt, SCORE_MATMUL_KEYS // t)
    group = math.gcd(n_kt // cw, SCORE_MATMULS_PER_TRIP)
    sub = math.gcd(n_kt, DIFF_QUERY_TILES_PER_STEP)
    tq = sub * t
    return pl.pallas_call(
        functools.partial(_diff_attn_body, t=t, sub=sub, cw=cw, n_kt=n_kt, group=group, lam_init=lam_init),
        grid=(b, h, s // tq),
        in_specs=[
            _smem_spec(),
            _smem_spec(),
            pl.BlockSpec((1, tq, w), lambda bi, hi, qi: (bi, qi, hi)),
            pl.BlockSpec((1, s, w), lambda bi, hi, qi: (bi, 0, h + hi)),
            pl.BlockSpec((1, s, w), lambda bi, hi, qi: (bi, 0, 2 * h + hi)),
            pl.BlockSpec((5, t, t), lambda bi, hi, qi: (0, 0, 0)),
            pl.BlockSpec((1, HEAD_DIM), lambda bi, hi, qi: (0, 0)),
            pl.BlockSpec((1, HEAD_DIM), lambda bi, hi, qi: (0, 0)),
            pl.BlockSpec((1, w), lambda bi, hi, qi: (0, 0)),
            pl.BlockSpec((4, HEAD_DIM), lambda bi, hi, qi: (0, 0)),
        ],
        out_specs=pl.BlockSpec((1, tq, w), lambda bi, hi, qi: (bi, qi, hi)),
        out_shape=jax.ShapeDtypeStruct((b, s, h * w), BF16),
        scratch_shapes=[
            pltpu.VMEM((2, HEAD_DIM, s), BF16), pltpu.VMEM((s, w), BF16), pltpu.VMEM((5, t, t), F32),
            pltpu.VMEM((2, tq, HEAD_DIM), BF16), pltpu.VMEM((2, tq, s), BF16), pltpu.VMEM((2, tq, LANES), F32),
            pltpu.VMEM((2, tq, 1), F32), pltpu.VMEM((2, tq, 1), F32), pltpu.VMEM((2, tq, w), F32),
        ],
        compiler_params=_cparams("arbitrary", "arbitrary", "arbitrary"),
        name="diff_attention",
    )(bounded, rel_table, proj, proj, proj, bucket_tiles, g_qa, g_ka, g_subln, lam_vecs)


def _gqa_body(flag_ref, q_ref, k_ref, v_ref, cq_ref, sq_ref, ck_ref, sk_ref, gq_ref, gk_ref, o_ref,
              kt_ref, va_ref, q_scr, p_ref, m_ref, l_ref, acc_ref, *, tq, tk, n_kt, group, rep):
    qi = pl.program_id(2)
    d = HEAD_DIM

    @pl.when(qi == 0)
    def _():
        kn = _rope(_rms(k_ref[0], gk_ref[...]), ck_ref[...], sk_ref[...])
        kt_ref[...] = kn.T.astype(BF16)
        va_ref[:, 0:d] = v_ref[0].astype(BF16)
        lane = lax.broadcasted_iota(jnp.int32, (va_ref.shape[0], d), 1)
        va_ref[:, d:2 * d] = jnp.where(lane == 0, 1.0, 0.0).astype(BF16)

    cq, sq = cq_ref[...], sq_ref[...]
    for r in range(rep):
        qr = _rope(_rms(q_ref[0, :, r * d:(r + 1) * d], gq_ref[...]), cq, sq)
        q_scr[r * tq:(r + 1) * tq, :] = (qr * (ATTN_SCALE * LOG2E)).astype(BF16)

    def tile_scores(kt):
        k0 = pl.multiple_of(kt * tk, tk)
        return jnp.dot(q_scr[...], kt_ref[:, pl.ds(k0, tk)], preferred_element_type=F32)

    def write(o):
        for r in range(rep):
            o_ref[0, :, r * d:(r + 1) * d] = o[r * tq:(r + 1) * tq].astype(o_ref.dtype)

    @pl.when(flag_ref[0] == 1)
    def _():
        def step(kg, c):
            for u in range(group):
                kt = kg * group + u
                k0 = pl.multiple_of(kt * tk, tk)
                p_ref[:, pl.ds(k0, tk)] = jnp.exp2(tile_scores(kt)).astype(BF16)
            return c
        lax.fori_loop(0, n_kt // group, step, 0)
        o = jnp.dot(p_ref[...], va_ref[...], preferred_element_type=F32)
        write(o[:, 0:d] / o[:, d:d + 1])

    @pl.when(flag_ref[0] != 1)
    def _():
        _reset_online(m_ref, l_ref, acc_ref)

        def step(kt, c):
            k0 = pl.multiple_of(kt * tk, tk)
            _online_softmax_step(tile_scores(kt), va_ref[pl.ds(k0, tk), 0:d], m_ref, l_ref, acc_ref)
            return c
        lax.fori_loop(0, n_kt, step, 0)
        write(acc_ref[...] / l_ref[...])


def gqa_attention(proj, bounded, cos, sin_signed, g_qb, g_kb, tq=256, tk=SCORE_MATMUL_KEYS):
    b, s, _ = proj.shape
    g = N_KV_B
    rep = N_HEADS_B // N_KV_B
    d = HEAD_DIM
    qw = rep * d
    tq, tk = min(tq, s), min(tk, s)
    a_w = N_HEADS_A * 2 * d
    q_blk0 = 3 * a_w // qw
    k_blk0 = (3 * a_w + N_HEADS_B * d) // d
    v_blk0 = k_blk0 + g
    rows = rep * tq
    return pl.pallas_call(
        functools.partial(_gqa_body, tq=tq, tk=tk, n_kt=s // tk,
                          group=math.gcd(s // tk, SCORE_MATMULS_PER_TRIP), rep=rep),
        grid=(b, g, s // tq),
        in_specs=[
            _smem_spec(),
            pl.BlockSpec((1, tq, qw), lambda bi, gi, qi: (bi, qi, q_blk0 + gi)),
            pl.BlockSpec((1, s, d), lambda bi, gi, qi: (bi, 0, k_blk0 + gi)),
            pl.BlockSpec((1, s, d), lambda bi, gi, qi: (bi, 0, v_blk0 + gi)),
            pl.BlockSpec((tq, d), lambda bi, gi, qi: (qi, 0)),
            pl.BlockSpec((tq, d), lambda bi, gi, qi: (qi, 0)),
            pl.BlockSpec((s, d), lambda bi, gi, qi: (0, 0)),
            pl.BlockSpec((s, d), lambda bi, gi, qi: (0, 0)),
            pl.BlockSpec((1, d), lambda bi, gi, qi: (0, 0)),
            pl.BlockSpec((1, d), lambda bi, gi, qi: (0, 0)),
        ],
        out_specs=pl.BlockSpec((1, tq, qw), lambda bi, gi, qi: (bi, qi, gi)),
        out_shape=jax.ShapeDtypeStruct((b, s, g * qw), BF16),
        scratch_shapes=[
            pltpu.VMEM((d, s), BF16), pltpu.VMEM((s, 2 * d), BF16),
            pltpu.VMEM((rows, d), BF16), pltpu.VMEM((rows, s), BF16),
            pltpu.VMEM((rows, 1), F32), pltpu.VMEM((rows, 1), F32), pltpu.VMEM((rows, d), F32),
        ],
        compiler_params=_cparams("arbitrary", "arbitrary", "arbitrary"),
        name="gqa_attention",
    )(bounded, proj, proj, proj, cos, sin_signed, cos, sin_signed, g_qb, g_kb)


def _cross_heads(qx, k_ref, v_ref, gq, gk):
    d = HEAD_DIM
    outs = []
    for hh in range(N_HEADS_X):
        sl = slice(hh * d, (hh + 1) * d)
        qn = (_rms(qx[:, sl], gq) * ATTN_SCALE).astype(BF16)
        kn = _rms(k_ref[0, :, sl], gk).astype(BF16)
        s = lax.dot_general(qn, kn, NT_DIMS, preferred_element_type=F32)
        p = jnp.exp(s - jnp.max(s, axis=-1, keepdims=True))
        o = jnp.dot(p.astype(BF16), v_ref[0, :, sl].astype(BF16), preferred_element_type=F32)
        outs.append((o / jnp.sum(p, axis=-1, keepdims=True)).astype(BF16))
    return jnp.concatenate(outs, axis=1)


def _split_bf16(x):
    hi = x.astype(BF16)
    lo = (x - hi.astype(F32)).astype(BF16)
    return hi, lo


def _cross_ffn_body(x_ref, k_ref, v_ref, gc_ref, wq_ref, gq_ref, gk_ref, wo_ref, gf_ref, wr_ref,
                    x2_ref, fp_ref, lg_ref):
    x = x_ref[...]
    c = _rms(x, gc_ref[...]).astype(BF16)
    qx = jnp.dot(c, wq_ref[...], preferred_element_type=F32)
    ox = _cross_heads(qx, k_ref, v_ref, gq_ref[...], gk_ref[...])
    x2 = x + jnp.dot(ox, wo_ref[...], preferred_element_type=F32)
    x2_ref[...] = x2
    f = _rms(x2, gf_ref[...])
    half = f.shape[1] // 2
    fr = f.astype(BF16).astype(F32)
    lo_bits = pltpu.bitcast(fr[:, :half], jnp.uint32) >> 16
    hi_bits = pltpu.bitcast(fr[:, half:], jnp.uint32) & jnp.uint32(0xFFFF0000)
    fp_ref[...] = lo_bits | hi_bits
    f_hi, f_lo = _split_bf16(f)
    w_hi, w_lo = _split_bf16(wr_ref[...])
    lg_ref[...] = (jnp.dot(f_hi, w_hi, preferred_element_type=F32)
                   + jnp.dot(f_hi, w_lo, preferred_element_type=F32)
                   + jnp.dot(f_lo, w_hi, preferred_element_type=F32))


def cross_ffn_block(x, kx, vx, g_cross, wq, g_qx, g_kx, wo, g_ffn, w_router, tm=256):
    n, d = x.shape
    b, n_mem, xw = kx.shape
    e = w_router.shape[1]
    tm = min(tm, n // b)
    tiles_per_seq = n // b // tm
    row = lambda i: (i, 0)
    fixed = lambda i: (0, 0)
    mem = lambda i: (i // tiles_per_seq, 0, 0)
    return pl.pallas_call(
        _cross_ffn_body,
        grid=(n // tm,),
        in_specs=[
            pl.BlockSpec((tm, d), row),
            pl.BlockSpec((1, n_mem, xw), mem),
            pl.BlockSpec((1, n_mem, xw), mem),
            pl.BlockSpec((1, d), fixed),
            pl.BlockSpec((d, xw), fixed),
            pl.BlockSpec((1, HEAD_DIM), fixed),
            pl.BlockSpec((1, HEAD_DIM), fixed),
            pl.BlockSpec((xw, d), fixed),
            pl.BlockSpec((1, d), fixed),
            pl.BlockSpec((d, e), fixed),
        ],
        out_specs=[pl.BlockSpec((tm, d), row), pl.BlockSpec((tm, d // 2), row), pl.BlockSpec((tm, e), row)],
        out_shape=[jax.ShapeDtypeStruct((n, d), F32), jax.ShapeDtypeStruct((n, d // 2), jnp.uint32),
                   jax.ShapeDtypeStruct((n, e), F32)],
        compiler_params=_cparams("arbitrary"),
        name="cross_ffn_block",
    )(x, kx, vx, g_cross.reshape(1, d), wq, g_qx, g_kx, wo, g_ffn.reshape(1, d), w_router)


def _unpack_rows(p_ref, o_ref, rows=256):
    m, half = p_ref.shape
    rows = min(rows, m)
    for r0 in range(0, m, rows):
        p = p_ref[r0:r0 + rows, :]
        o_ref[r0:r0 + rows, :half] = pltpu.bitcast(p << 16, F32).astype(BF16)
        o_ref[r0:r0 + rows, half:] = pltpu.bitcast(p & jnp.uint32(0xFFFF0000), F32).astype(BF16)


def _select_top_cap(lg_ref, aff_ref, sel_ref, cs_ref, *, n, cap):
    e = lg_ref.shape[0]
    lg = lg_ref[...]
    ex = jnp.exp(lg - jnp.max(lg, axis=0, keepdims=True))
    aff = ex / jnp.sum(ex, axis=0, keepdims=True)
    aff_ref[...] = aff
    bits = pltpu.bitcast(aff, jnp.int32)

    def search(i, prefix):
        cand = prefix | (jnp.int32(1) << (30 - i))
        cnt = jnp.sum(jnp.where(bits >= cand, 1.0, 0.0), axis=1, keepdims=True)
        return jnp.where(cnt >= cap, cand, prefix)

    thr = lax.fori_loop(0, 31, search, jnp.zeros((e, 1), jnp.int32))
    gt = bits > thr
    need = cap - jnp.sum(jnp.where(gt, 1.0, 0.0), axis=1, keepdims=True)

    tri = (lax.broadcasted_iota(jnp.int32, (LANES, LANES), 0)
           <= lax.broadcasted_iota(jnp.int32, (LANES, LANES), 1)).astype(BF16)

    def cumsum_lanes(ref):
        def blk(j, carry):
            c0 = pl.multiple_of(j * LANES, LANES)
            w = jnp.dot(ref[:, pl.ds(c0, LANES)].astype(BF16), tri, preferred_element_type=F32) + carry
            ref[:, pl.ds(c0, LANES)] = w
            return w[:, LANES - 1:LANES]
        lax.fori_loop(0, n // LANES, blk, jnp.zeros((e, 1), F32))

    eq = bits == thr
    cs_ref[...] = jnp.where(eq, 1.0, 0.0)
    cumsum_lanes(cs_ref)
    sel = gt | (eq & (cs_ref[...] <= need))
    sel_ref[...] = jnp.where(sel, 1.0, 0.0)
    cs_ref[...] = sel_ref[...]
    cumsum_lanes(cs_ref)
    cs_ref[...] = cs_ref[...] * sel_ref[...]


def _route_body(lg_ref, idx_ref, gate_ref, aff_ref, sel_ref, cs_ref, aff_rows, cs_rows, *, n, cap, chunk):
    ei = pl.program_id(0)

    @pl.when(ei == 0)
    def _():
        _select_top_cap(lg_ref, aff_ref, sel_ref, cs_ref, n=n, cap=cap)
        for k in range(lg_ref.shape[0]):
            aff_rows[k] = aff_ref[k:k + 1, :]
            cs_rows[k] = cs_ref[k:k + 1, :]

    tok = lax.broadcasted_iota(jnp.int32, (1, n), 1).astype(F32)
    lane = lax.broadcasted_iota(jnp.int32, (chunk, LANES), 1).astype(F32)
    sub = lax.broadcasted_iota(jnp.int32, (chunk, 1), 0).astype(F32)

    def per_chunk(ci, carry):
        s0 = pl.multiple_of(ci * chunk, chunk)
        lo = s0.astype(F32)
        row = cs_rows[ei]
        inside = jnp.abs(row - (lo + 0.5 * (chunk + 1))) < 0.5 * chunk
        t_first = jnp.min(jnp.where(inside, tok, float(n)), axis=1, keepdims=True)[0, 0]
        t_last = jnp.max(jnp.where(inside, tok, -1.0), axis=1, keepdims=True)[0, 0]
        j_lo = t_first.astype(jnp.int32) // LANES
        j_hi = t_last.astype(jnp.int32) // LANES + 1
        want = sub + (lo + 1.0)

        def per_blk(j, accs):
            acc_i, acc_g = accs
            c0 = pl.multiple_of(j * LANES, LANES)
            hit = cs_rows[ei, :, pl.ds(c0, LANES)] == want
            acc_i = acc_i + jnp.where(hit, lane + c0.astype(F32), 0.0)
            acc_g = acc_g + jnp.where(hit, aff_rows[ei, :, pl.ds(c0, LANES)], 0.0)
            return acc_i, acc_g

        z = jnp.zeros((chunk, LANES), F32)
        acc_i, acc_g = lax.fori_loop(j_lo, j_hi, per_blk, (z, z))
        idx_ref[0, pl.ds(s0, chunk), :] = jnp.sum(acc_i, axis=1, keepdims=True).astype(jnp.int32)
        gate_ref[0, pl.ds(s0, chunk), :] = jnp.sum(acc_g, axis=1, keepdims=True)
        return carry

    lax.fori_loop(0, cap // chunk, per_chunk, 0)


def route(logits_t, cap):
    e, n = logits_t.shape
    chunk = min(LANES, cap)
    return pl.pallas_call(
        functools.partial(_route_body, n=n, cap=cap, chunk=chunk),
        grid=(e,),
        in_specs=[pl.BlockSpec((e, n), lambda i: (0, 0))],
        out_specs=[pl.BlockSpec((1, cap, 1), lambda i: (i, 0, 0)), pl.BlockSpec((1, cap, 1), lambda i: (i, 0, 0))],
        out_shape=[jax.ShapeDtypeStruct((e, cap, 1), jnp.int32), jax.ShapeDtypeStruct((e, cap, 1), F32)],
        scratch_shapes=[pltpu.VMEM((e, n), F32), pltpu.VMEM((e, n), F32), pltpu.VMEM((e, n), F32),
                        pltpu.VMEM((e, 1, n), F32), pltpu.VMEM((e, 1, n), F32)],
        compiler_params=_cparams("arbitrary"),
        name="route",
    )(logits_t)


def _row_copy(src_ref, dst_ref, src_row, dst_row, sem):
    return pltpu.make_async_copy(src_ref.at[pl.ds(src_row, 1)], dst_ref.at[pl.ds(dst_row, 1)], sem)


def _wait_rows(src_ref, dst_ref, n, sem):
    pltpu.make_async_copy(src_ref.at[pl.ds(0, n)], dst_ref.at[pl.ds(0, n)], sem).wait()


ROW_DMA_UNROLL = 8


def _gate_up_body(idx_ref, f_ref, wg_ref, wu_ref, o_ref, xg_ref, xb_ref, sems, *, tm, n_f, n_tiles):
    fi = pl.program_id(2)
    tile = pl.program_id(0) * pl.num_programs(1) + pl.program_id(1)
    slot = tile % 2
    share = tm // n_f

    def start_row(tile_id, slot_id, r):
        _row_copy(f_ref, xg_ref.at[slot_id], idx_ref[tile_id * tm + r], r, sems.at[slot_id]).start()

    @pl.when((tile == 0) & (fi == 0))
    def _():
        def issue(r, c):
            start_row(0, 0, r)
            return c
        lax.fori_loop(0, tm, issue, 0, unroll=ROW_DMA_UNROLL)

    @pl.when(fi == 0)
    def _():
        _wait_rows(f_ref, xg_ref.at[slot], tm, sems.at[slot])
        _unpack_rows(xg_ref.at[slot], xb_ref)

    nxt = (tile + 1) % n_tiles
    for r in range(share):
        start_row(nxt, 1 - slot, fi * share + r)

    x = xb_ref[...]
    g = jnp.dot(x, wg_ref[0].astype(BF16), preferred_element_type=F32)
    u = jnp.dot(x, wu_ref[0].astype(BF16), preferred_element_type=F32)
    o_ref[0] = (g * (1.0 / (1.0 + jnp.exp(-g))) * u).astype(o_ref.dtype)

    @pl.when((tile == n_tiles - 1) & (fi == n_f - 1))
    def _():
        _wait_rows(f_ref, xg_ref.at[1 - slot], tm, sems.at[1 - slot])


def expert_gate_up(f_packed, idx, e, w_gate, w_up, tm=1024, tf=256):
    c = idx.shape[0] // e
    dh = f_packed.shape[1]
    d = 2 * dh
    f = w_gate.shape[2]
    tm, tf = min(tm, c), min(tf, f)
    n_f = f // tf
    assert c % tm == 0 and tm % n_f == 0
    return pl.pallas_call(
        functools.partial(_gate_up_body, tm=tm, n_f=n_f, n_tiles=e * (c // tm)),
        grid_spec=pltpu.PrefetchScalarGridSpec(
            num_scalar_prefetch=1,
            grid=(e, c // tm, n_f),
            in_specs=[
                pl.BlockSpec(memory_space=pl.ANY),
                pl.BlockSpec((1, d, tf), lambda ei, mi, fi, idx_ref: (ei, 0, fi)),
                pl.BlockSpec((1, d, tf), lambda ei, mi, fi, idx_ref: (ei, 0, fi)),
            ],
            out_specs=pl.BlockSpec((1, tm, tf), lambda ei, mi, fi, idx_ref: (ei, mi, fi)),
            scratch_shapes=[pltpu.VMEM((2, tm, dh), jnp.uint32), pltpu.VMEM((tm, d), BF16),
                            pltpu.SemaphoreType.DMA((2,))],
        ),
        out_shape=jax.ShapeDtypeStruct((e, c, f), BF16),
        compiler_params=_cparams("arbitrary", "arbitrary", "arbitrary"),
        name="expert_gate_up",
    )(idx, f_packed, w_gate, w_up)


def _down_body(h_ref, w_ref, gate_ref, o_ref):
    y = jnp.dot(h_ref[0], w_ref[0].astype(BF16), preferred_element_type=F32) * gate_ref[0]
    o_ref[0] = y.astype(o_ref.dtype)


def expert_down(hid, w_down, gates, tm=1024, tn=512):
    e, c, f = hid.shape
    d = w_down.shape[2]
    tm, tn = min(tm, c), min(tn, d)
    return pl.pallas_call(
        _down_body,
        grid=(e, c // tm, d // tn),
        in_specs=[
            pl.BlockSpec((1, tm, f), lambda ei, mi, ni: (ei, mi, 0)),
            pl.BlockSpec((1, f, tn), lambda ei, mi, ni: (ei, 0, ni)),
            pl.BlockSpec((1, tm, 1), lambda ei, mi, ni: (ei, mi, 0)),
        ],
        out_specs=pl.BlockSpec((1, tm, tn), lambda ei, mi, ni: (ei, mi, ni)),
        out_shape=jax.ShapeDtypeStruct((e, c, d), BF16),
        compiler_params=_cparams("arbitrary", "arbitrary", "arbitrary"),
        name="expert_down",
    )(hid, w_down, gates)


COMBINE_SLOTS = 3


def _combine_body(idx_ref, ye_ref, x_ref, o_ref, buf_ref, sem_in, sem_out, *, rows, tpe, n_tiles):
    del x_ref
    i = pl.program_id(0)
    slot = i % COMBINE_SLOTS
    first = (i % tpe) == 0
    last = (i % tpe) == tpe - 1

    def fetch(tile, s):
        def one(r, c):
            _row_copy(o_ref, buf_ref.at[s], idx_ref[tile * rows + r], r, sem_in.at[s]).start()
            return c
        lax.fori_loop(0, rows, one, 0, unroll=ROW_DMA_UNROLL)

    def put(tile, s):
        def one(r, c):
            _row_copy(buf_ref.at[s], o_ref, r, idx_ref[tile * rows + r], sem_out.at[s]).start()
            return c
        lax.fori_loop(0, rows, one, 0, unroll=ROW_DMA_UNROLL)

    def wait_put(s):
        _wait_rows(buf_ref.at[s], o_ref, rows, sem_out.at[s])

    @pl.when((i >= 2) & (((i - 1) % tpe) != 0))
    def _():
        wait_put((i - 2) % COMBINE_SLOTS)

    @pl.when(first)
    def _():
        @pl.when(i >= 1)
        def _():
            wait_put((i - 1) % COMBINE_SLOTS)
        fetch(i, slot)

    @pl.when(jnp.logical_not(last))
    def _():
        fetch(i + 1, (i + 1) % COMBINE_SLOTS)

    _wait_rows(o_ref, buf_ref.at[slot], rows, sem_in.at[slot])
    buf_ref[slot] = buf_ref[slot] + ye_ref[...].astype(F32)
    put(i, slot)

    @pl.when(i == n_tiles - 1)
    def _():
        if n_tiles >= 2 and (n_tiles - 1) % tpe != 0:
            wait_put((n_tiles - 2) % COMBINE_SLOTS)
        wait_put((n_tiles - 1) % COMBINE_SLOTS)


def combine(x, ye, idx, rows, rows_per_expert):
    r, d = ye.shape
    assert rows_per_expert % rows == 0
    return pl.pallas_call(
        functools.partial(_combine_body, rows=rows, tpe=rows_per_expert // rows, n_tiles=r // rows),
        grid_spec=pltpu.PrefetchScalarGridSpec(
            num_scalar_prefetch=1,
            grid=(r // rows,),
            in_specs=[pl.BlockSpec((rows, d), lambda i, idx_ref: (i, 0)), pl.BlockSpec(memory_space=pl.ANY)],
            out_specs=pl.BlockSpec(memory_space=pl.ANY),
            scratch_shapes=[pltpu.VMEM((COMBINE_SLOTS, rows, d), F32), pltpu.SemaphoreType.DMA((COMBINE_SLOTS,)),
                            pltpu.SemaphoreType.DMA((COMBINE_SLOTS,))],
        ),
        out_shape=jax.ShapeDtypeStruct(x.shape, x.dtype),
        input_output_aliases={2: 0},
        compiler_params=_cparams("arbitrary"),
        name="combine",
    )(idx, ye, x)


def _rel_bucket(rel):
    nb = REL_BUCKETS // 2
    max_exact = nb // 2
    ret = jnp.where(rel > 0, nb, 0)
    n = jnp.abs(rel)
    nf = jnp.maximum(n, 1).astype(F32)
    large = max_exact + (jnp.log(nf / max_exact) / math.log(REL_MAX_DIST / max_exact)
                         * (nb - max_exact)).astype(jnp.int32)
    large = jnp.minimum(large, nb - 1)
    return ret + jnp.where(n < max_exact, n, large)


def _bucket_tiles(t):
    r = np.arange(t)[:, None]
    c = np.arange(t)[None, :]
    rel = np.stack([dt * t + (c - r) for dt in (-2, -1, 0, 1, 2)]).astype(np.int32)
    return _rel_bucket(jnp.asarray(rel)).astype(jnp.int32)


def _axial_rope_tables(s):
    rows = s // GRID_W
    row = jnp.repeat(jnp.arange(rows), GRID_W).astype(F32)
    col = jnp.tile(jnp.arange(GRID_W), rows).astype(F32)
    half = HEAD_DIM // 2
    inv = 1.0 / (ROPE_THETA ** (jnp.arange(0, half, 2, dtype=F32) / half))
    ang_r = row[:, None] * inv
    ang_c = col[:, None] * inv
    ang = jnp.concatenate([ang_r, ang_r, ang_c, ang_c], axis=-1)
    sign = np.where((np.arange(HEAD_DIM) % half) < half // 2, -1.0, 1.0).astype(np.float32)
    return jnp.cos(ang), jnp.sin(ang) * sign


def _layer(x, mem, rel_table, g_mix, w_in, g_qa, g_ka, lam_vecs, g_subln, g_qb, g_kb, w_o, g_cross,
           g_mem, wq_x, wk_x, wv_x, g_qx, g_kx, wo_x, g_ffn, w_router, w_gate, w_up, w_down, lam_init):
    b, s, dm = x.shape
    n = b * s
    n_mem = mem.shape[1]
    x2d = x.reshape(n, dm)
    row = lambda v: v.reshape(1, -1)

    h = rmsnorm_cast(x2d, g_mix)
    proj = matmul([h], w_in, out_dtype=BF16).reshape(b, s, -1)
    t_a = min(256, s)
    assert s % t_a == 0 and t_a >= REL_MAX_DIST, "bias tiles need saturated buckets two tiles away"
    bounded_a = _bounded_flag(_score_bound(g_qa, g_ka, jnp.max(jnp.abs(rel_table))))
    oa = diff_attention(proj, bounded_a, rel_table, _bucket_tiles(t_a), row(g_qa), row(g_ka), row(g_subln),
                        lam_vecs, lam_init, t_a)
    cos, sin_signed = _axial_rope_tables(s)
    ob = gqa_attention(proj, _bounded_flag(_score_bound(g_qb, g_kb)), cos, sin_signed, row(g_qb), row(g_kb))
    x1 = matmul([oa.reshape(n, -1), ob.reshape(n, -1)], w_o, res=x2d, tm=512)

    m = rmsnorm_cast(mem.reshape(b * n_mem, dm), g_mem)
    kx = matmul([m], wk_x).reshape(b, n_mem, -1)
    vx = matmul([m], wv_x).reshape(b, n_mem, -1)
    x2, f_packed, logits = cross_ffn_block(x1, kx, vx, g_cross, wq_x.astype(BF16), row(g_qx), row(g_kx),
                                           wo_x.astype(BF16), g_ffn, w_router)

    e = w_router.shape[1]
    cap = EC_FACTOR * n // e
    idx, gates = route(logits.T, cap)
    idx_flat = idx.reshape(e * cap)
    hid = expert_gate_up(f_packed, idx_flat, e, w_gate, w_up)
    ye = expert_down(hid, w_down, gates)
    out = combine(x2, ye.reshape(e * cap, dm), idx_flat, rows=min(256, cap), rows_per_expert=cap)
    return out.reshape(b, s, dm)


def kernel(x_prompt, x_sample, mem_prompt, mem_sample, rel_table, g_mix, w_in, g_qa, g_ka, lam_q1, lam_k1, lam_q2, lam_k2, g_subln, g_qb, g_kb, w_o, g_cross, g_mem, wq_x, wk_x, wv_x, g_qx, g_kx, wo_x, g_ffn, w_router, w_gate, w_up, w_down):
    outs = []
    for x, mem in ((x_prompt, mem_prompt), (x_sample, mem_sample)):
        for l in range(g_mix.shape[0]):
            lam_init = 0.8 - 0.6 * math.exp(-0.3 * l)
            lam_vecs = jnp.stack([lam_q1[l], lam_k1[l], lam_q2[l], lam_k2[l]])
            x = _layer(x, mem, rel_table, g_mix[l], w_in[l], g_qa[l], g_ka[l], lam_vecs, g_subln[l], g_qb[l],
                       g_kb[l], w_o[l], g_cross[l], g_mem[l], wq_x[l], wk_x[l], wv_x[l], g_qx[l], g_kx[l],
                       wo_x[l], g_ffn[l], w_router[l], w_gate[l], w_up[l], w_down[l], lam_init)
        outs.append(x)
    return tuple(outs)
```

```python
import functools
import math

import numpy as np
import jax
import jax.numpy as jnp
from jax import lax
from jax.experimental import pallas as pl
from jax.experimental.pallas import tpu as pltpu

F32 = jnp.float32
BF16 = jnp.bfloat16

HEAD_DIM = 128
N_HEADS_A = 8
N_HEADS_B = 16
N_KV_B = 4
N_HEADS_X = 4
EC_FACTOR = 2
GRID_W = 64
REL_BUCKETS = 32
REL_MAX_DIST = 128
ROPE_THETA = 10000.0
EPS = 1e-6
ATTN_SCALE = HEAD_DIM ** -0.5
LOG2E = math.log2(math.e)

V7X_VMEM_BYTES = 64 * 1024 * 1024
VMEM_LIMIT = V7X_VMEM_BYTES - 8 * 1024 * 1024
LANES = 128
SUBLANES = 8

SAFE_SCORE_BOUND = 30.0
BF16_NORM_SLACK = (1.0 + 2.0 ** -8) ** 2
SCORE_MATMUL_KEYS = 1024
SCORE_MATMULS_PER_TRIP = 4
DIFF_QUERY_TILES_PER_STEP = 4

NT_DIMS = (((1,), (1,)), ((), ()))


def _cparams(*sem):
    return pltpu.CompilerParams(dimension_semantics=sem, vmem_limit_bytes=VMEM_LIMIT)


def _rms(x, g):
    x = x.astype(F32)
    return x * lax.rsqrt(jnp.mean(x * x, axis=-1, keepdims=True) + EPS) * g


def _smem_spec():
    return pl.BlockSpec(memory_space=pltpu.SMEM)


def _rmsnorm_body(x_ref, g_ref, o_ref):
    o_ref[...] = _rms(x_ref[...], g_ref[...]).astype(o_ref.dtype)


def rmsnorm_cast(x, g, tm=256):
    m, d = x.shape
    tm = min(tm, m)
    return pl.pallas_call(
        _rmsnorm_body,
        grid=(m // tm,),
        in_specs=[pl.BlockSpec((tm, d), lambda i: (i, 0)), pl.BlockSpec((1, d), lambda i: (0, 0))],
        out_specs=pl.BlockSpec((tm, d), lambda i: (i, 0)),
        out_shape=jax.ShapeDtypeStruct((m, d), BF16),
        compiler_params=_cparams("arbitrary"),
        name="rmsnorm_cast",
    )(x, g.reshape(1, d))


def _mm_body(*refs, n_a, has_res):
    a_refs, w_ref = refs[:n_a], refs[n_a]
    r_ref = refs[n_a + 1] if has_res else None
    o_ref, wb_ref = refs[-2:]

    @pl.when(pl.program_id(1) == 0)
    def _():
        wb_ref[...] = w_ref[...].astype(BF16)

    acc, k0 = None, 0
    for a_ref in a_refs:
        k = a_ref.shape[1]
        part = jnp.dot(a_ref[...], wb_ref[k0:k0 + k, :], preferred_element_type=F32)
        acc = part if acc is None else acc + part
        k0 += k
    if has_res:
        acc = acc + r_ref[...]
    o_ref[...] = acc.astype(o_ref.dtype)


def matmul(a_list, w, res=None, out_dtype=F32, tm=1024, tn=1024):
    m = a_list[0].shape[0]
    k, n = w.shape
    assert sum(a.shape[1] for a in a_list) == k
    tm, tn = min(tm, m), min(tn, n)
    in_specs = [pl.BlockSpec((tm, a.shape[1]), lambda j, i: (i, 0)) for a in a_list]
    in_specs.append(pl.BlockSpec((k, tn), lambda j, i: (0, j), pipeline_mode=pl.Buffered(1)))
    args = list(a_list) + [w]
    if res is not None:
        in_specs.append(pl.BlockSpec((tm, tn), lambda j, i: (i, j)))
        args.append(res)
    return pl.pallas_call(
        functools.partial(_mm_body, n_a=len(a_list), has_res=res is not None),
        grid=(n // tn, m // tm),
        in_specs=in_specs,
        out_specs=pl.BlockSpec((tm, tn), lambda j, i: (i, j)),
        out_shape=jax.ShapeDtypeStruct((m, n), out_dtype),
        scratch_shapes=[pltpu.VMEM((k, tn), BF16)],
        compiler_params=_cparams("arbitrary", "arbitrary"),
        name="matmul",
    )(*args)


def _online_softmax_step(s, v, m_ref, l_ref, acc_ref):
    m_old = m_ref[...]
    m_new = jnp.maximum(m_old, jnp.max(s, axis=-1, keepdims=True))
    alpha = jnp.exp2(m_old - m_new)
    p = jnp.exp2(s - m_new)
    l_ref[...] = alpha * l_ref[...] + jnp.sum(p, axis=-1, keepdims=True)
    acc_ref[...] = alpha * acc_ref[...] + jnp.dot(p.astype(BF16), v, preferred_element_type=F32)
    m_ref[...] = m_new


def _reset_online(m_ref, l_ref, acc_ref):
    m_ref[...] = jnp.full(m_ref.shape, -jnp.inf, F32)
    l_ref[...] = jnp.zeros(l_ref.shape, F32)
    acc_ref[...] = jnp.zeros(acc_ref.shape, F32)


def _rope(x, cos, sin_signed):
    lane = lax.broadcasted_iota(jnp.int32, x.shape, 1)
    first = (lane % (HEAD_DIM // 2)) < (HEAD_DIM // 4)
    xr = jnp.where(first, pltpu.roll(x, HEAD_DIM - HEAD_DIM // 4, 1), pltpu.roll(x, HEAD_DIM // 4, 1))
    return x * cos + xr * sin_signed


def _score_bound(g_q, g_k, extra=0.0):
    return HEAD_DIM * ATTN_SCALE * BF16_NORM_SLACK * jnp.max(jnp.abs(g_q)) * jnp.max(jnp.abs(g_k)) + extra


def _bounded_flag(bound):
    return (bound <= SAFE_SCORE_BOUND).astype(jnp.int32).reshape(1)


def _diff_attn_body(flag_ref, tab_ref, q_ref, k_ref, v_ref, bkt_ref, gq_ref, gk_ref, gs_ref, lam_ref, o_ref,
                    kt_ref, vb_ref, bias_ref, q_scr, p_ref, lsum_ref, m_ref, l_ref, acc_ref,
                    *, t, sub, cw, n_kt, group, lam_init):
    hi = pl.program_id(1)
    qi = pl.program_id(2)
    d = HEAD_DIM

    @pl.when(qi == 0)
    def _():
        for j in range(2):
            kj = k_ref[0, :, j * d:(j + 1) * d]
            kt_ref[j] = _rms(kj, gk_ref[...]).T.astype(BF16)
        vb_ref[...] = v_ref[0].astype(BF16)

        def build(ti, c):
            bk = bkt_ref[ti]
            acc = jnp.zeros((t, t), F32)
            for b in range(REL_BUCKETS):
                acc = jnp.where(bk == b, tab_ref[b, hi], acc)
            bias_ref[ti] = acc * LOG2E
            return c
        lax.fori_loop(0, 5, build, 0)

    for j in range(2):
        qj = q_ref[0, :, j * d:(j + 1) * d]
        q_scr[j] = (_rms(qj, gq_ref[...]) * (ATTN_SCALE * LOG2E)).astype(BF16)

    def finish(o1, o2):
        lam_v = lam_ref[...]
        lam = (jnp.exp(jnp.sum(lam_v[0:1] * lam_v[1:2], axis=-1, keepdims=True))
               - jnp.exp(jnp.sum(lam_v[2:3] * lam_v[3:4], axis=-1, keepdims=True)) + lam_init)
        o = o1 - lam * o2
        o_ref[0] = (_rms(o, gs_ref[...]) * (1.0 - lam_init)).astype(o_ref.dtype)

    kw = cw * t

    def block_scores(kb, j):
        k0 = pl.multiple_of(kb * kw, kw)
        s = jnp.dot(q_scr[j], kt_ref[j, :, pl.ds(k0, kw)], preferred_element_type=F32)
        rows = []
        for u in range(sub):
            cols = [s[u * t:(u + 1) * t, v * t:(v + 1) * t]
                    + bias_ref[jnp.clip(kb * cw + v - (qi * sub + u), -2, 2) + 2] for v in range(cw)]
            rows.append(cols[0] if cw == 1 else jnp.concatenate(cols, axis=1))
        return rows[0] if sub == 1 else jnp.concatenate(rows, axis=0)

    @pl.when(flag_ref[0] == 1)
    def _():
        lsum_ref[...] = jnp.zeros(lsum_ref.shape, F32)

        def step(kg, c):
            parts = [None, None]
            for u in range(group):
                kb = kg * group + u
                k0 = pl.multiple_of(kb * kw, kw)
                for j in range(2):
                    p = jnp.exp2(block_scores(kb, j))
                    p_ref[j, :, pl.ds(k0, kw)] = p.astype(BF16)
                    for c0 in range(0, kw, LANES):
                        pc = p[:, c0:c0 + LANES]
                        parts[j] = pc if parts[j] is None else parts[j] + pc
            for j in range(2):
                lsum_ref[j] += parts[j]
            return c
        lax.fori_loop(0, n_kt // (cw * group), step, 0)
        outs = []
        for j in range(2):
            l = jnp.sum(lsum_ref[j], axis=-1, keepdims=True)
            outs.append(jnp.dot(p_ref[j], vb_ref[...], preferred_element_type=F32) / l)
        finish(outs[0], outs[1])

    @pl.when(flag_ref[0] != 1)
    def _():
        _reset_online(m_ref, l_ref, acc_ref)

        def step(kb, c):
            k0 = pl.multiple_of(kb * kw, kw)
            vt = vb_ref[pl.ds(k0, kw), :]
            for j in range(2):
                _online_softmax_step(block_scores(kb, j), vt, m_ref.at[j], l_ref.at[j], acc_ref.at[j])
            return c
        lax.fori_loop(0, n_kt // cw, step, 0)
        finish(acc_ref[0] / l_ref[0], acc_ref[1] / l_ref[1])


def diff_attention(proj, bounded, rel_table, bucket_tiles, g_qa, g_ka, g_subln, lam_vecs, lam_init, t):
    b, s, _ = proj.shape
    h = N_HEADS_A
    w = 2 * HEAD_DIM
    n_kt = s // t
    cw = math.gcd(n_kt, SCORE_MATMUL_KEYS // t)
    group = math.gcd(n_kt // cw, SCORE_MATMULS_PER_TRIP)
    sub = math.gcd(n_kt, DIFF_QUERY_TILES_PER_STEP)
    tq = sub * t
    return pl.pallas_call(
        functools.partial(_diff_attn_body, t=t, sub=sub, cw=cw, n_kt=n_kt, group=group, lam_init=lam_init),
        grid=(b, h, s // tq),
        in_specs=[
            _smem_spec(),
            _smem_spec(),
            pl.BlockSpec((1, tq, w), lambda bi, hi, qi: (bi, qi, hi)),
            pl.BlockSpec((1, s, w), lambda bi, hi, qi: (bi, 0, h + hi)),
            pl.BlockSpec((1, s, w), lambda bi, hi, qi: (bi, 0, 2 * h + hi)),
            pl.BlockSpec((5, t, t), lambda bi, hi, qi: (0, 0, 0)),
            pl.BlockSpec((1, HEAD_DIM), lambda bi, hi, qi: (0, 0)),
            pl.BlockSpec((1, HEAD_DIM), lambda bi, hi, qi: (0, 0)),
            pl.BlockSpec((1, w), lambda bi, hi, qi: (0, 0)),
            pl.BlockSpec((4, HEAD_DIM), lambda bi, hi, qi: (0, 0)),
        ],
        out_specs=pl.BlockSpec((1, tq, w), lambda bi, hi, qi: (bi, qi, hi)),
        out_shape=jax.ShapeDtypeStruct((b, s, h * w), BF16),
        scratch_shapes=[
            pltpu.VMEM((2, HEAD_DIM, s), BF16), pltpu.VMEM((s, w), BF16), pltpu.VMEM((5, t, t), F32),
            pltpu.VMEM((2, tq, HEAD_DIM), BF16), pltpu.VMEM((2, tq, s), BF16), pltpu.VMEM((2, tq, LANES), F32),
            pltpu.VMEM((2, tq, 1), F32), pltpu.VMEM((2, tq, 1), F32), pltpu.VMEM((2, tq, w), F32),
        ],
        compiler_params=_cparams("arbitrary", "arbitrary", "arbitrary"),
        name="diff_attention",
    )(bounded, rel_table, proj, proj, proj, bucket_tiles, g_qa, g_ka, g_subln, lam_vecs)


def _gqa_body(flag_ref, q_ref, k_ref, v_ref, cq_ref, sq_ref, ck_ref, sk_ref, gq_ref, gk_ref, o_ref,
              kt_ref, va_ref, q_scr, p_ref, m_ref, l_ref, acc_ref, *, tq, tk, n_kt, group, rep):
    qi = pl.program_id(2)
    d = HEAD_DIM

    @pl.when(qi == 0)
    def _():
        kn = _rope(_rms(k_ref[0], gk_ref[...]), ck_ref[...], sk_ref[...])
        kt_ref[...] = kn.T.astype(BF16)
        va_ref[:, 0:d] = v_ref[0].astype(BF16)
        lane = lax.broadcasted_iota(jnp.int32, (va_ref.shape[0], d), 1)
        va_ref[:, d:2 * d] = jnp.where(lane == 0, 1.0, 0.0).astype(BF16)

    cq, sq = cq_ref[...], sq_ref[...]
    for r in range(rep):
        qr = _rope(_rms(q_ref[0, :, r * d:(r + 1) * d], gq_ref[...]), cq, sq)
        q_scr[r * tq:(r + 1) * tq, :] = (qr * (ATTN_SCALE * LOG2E)).astype(BF16)

    def tile_scores(kt):
        k0 = pl.multiple_of(kt * tk, tk)
        return jnp.dot(q_scr[...], kt_ref[:, pl.ds(k0, tk)], preferred_element_type=F32)

    def write(o):
        for r in range(rep):
            o_ref[0, :, r * d:(r + 1) * d] = o[r * tq:(r + 1) * tq].astype(o_ref.dtype)

    @pl.when(flag_ref[0] == 1)
    def _():
        def step(kg, c):
            for u in range(group):
                kt = kg * group + u
                k0 = pl.multiple_of(kt * tk, tk)
                p_ref[:, pl.ds(k0, tk)] = jnp.exp2(tile_scores(kt)).astype(BF16)
            return c
        lax.fori_loop(0, n_kt // group, step, 0)
        o = jnp.dot(p_ref[...], va_ref[...], preferred_element_type=F32)
        write(o[:, 0:d] / o[:, d:d + 1])

    @pl.when(flag_ref[0] != 1)
    def _():
        _reset_online(m_ref, l_ref, acc_ref)

        def step(kt, c):
            k0 = pl.multiple_of(kt * tk, tk)
            _online_softmax_step(tile_scores(kt), va_ref[pl.ds(k0, tk), 0:d], m_ref, l_ref, acc_ref)
            return c
        lax.fori_loop(0, n_kt, step, 0)
        write(acc_ref[...] / l_ref[...])


def gqa_attention(proj, bounded, cos, sin_signed, g_qb, g_kb, tq=512, tk=SCORE_MATMUL_KEYS):
    b, s, _ = proj.shape
    g = N_KV_B
    rep = N_HEADS_B // N_KV_B
    d = HEAD_DIM
    qw = rep * d
    tq, tk = min(tq, s), min(tk, s)
    a_w = N_HEADS_A * 2 * d
    q_blk0 = 3 * a_w // qw
    k_blk0 = (3 * a_w + N_HEADS_B * d) // d
    v_blk0 = k_blk0 + g
    rows = rep * tq
    return pl.pallas_call(
        functools.partial(_gqa_body, tq=tq, tk=tk, n_kt=s // tk,
                          group=math.gcd(s // tk, SCORE_MATMULS_PER_TRIP), rep=rep),
        grid=(b, g, s // tq),
        in_specs=[
            _smem_spec(),
            pl.BlockSpec((1, tq, qw), lambda bi, gi, qi: (bi, qi, q_blk0 + gi)),
            pl.BlockSpec((1, s, d), lambda bi, gi, qi: (bi, 0, k_blk0 + gi)),
            pl.BlockSpec((1, s, d), lambda bi, gi, qi: (bi, 0, v_blk0 + gi)),
            pl.BlockSpec((tq, d), lambda bi, gi, qi: (qi, 0)),
            pl.BlockSpec((tq, d), lambda bi, gi, qi: (qi, 0)),
            pl.BlockSpec((s, d), lambda bi, gi, qi: (0, 0)),
            pl.BlockSpec((s, d), lambda bi, gi, qi: (0, 0)),
            pl.BlockSpec((1, d), lambda bi, gi, qi: (0, 0)),
            pl.BlockSpec((1, d), lambda bi, gi, qi: (0, 0)),
        ],
        out_specs=pl.BlockSpec((1, tq, qw), lambda bi, gi, qi: (bi, qi, gi)),
        out_shape=jax.ShapeDtypeStruct((b, s, g * qw), BF16),
        scratch_shapes=[
            pltpu.VMEM((d, s), BF16), pltpu.VMEM((s, 2 * d), BF16),
            pltpu.VMEM((rows, d), BF16), pltpu.VMEM((rows, s), BF16),
            pltpu.VMEM((rows, 1), F32), pltpu.VMEM((rows, 1), F32), pltpu.VMEM((rows, d), F32),
        ],
        compiler_params=_cparams("arbitrary", "arbitrary", "arbitrary"),
        name="gqa_attention",
    )(bounded, proj, proj, proj, cos, sin_signed, cos, sin_signed, g_qb, g_kb)


def _cross_heads(qx, k_ref, v_ref, gq, gk):
    d = HEAD_DIM
    outs = []
    for hh in range(N_HEADS_X):
        sl = slice(hh * d, (hh + 1) * d)
        qn = (_rms(qx[:, sl], gq) * ATTN_SCALE).astype(BF16)
        kn = _rms(k_ref[0, :, sl], gk).astype(BF16)
        s = lax.dot_general(qn, kn, NT_DIMS, preferred_element_type=F32)
        p = jnp.exp(s - jnp.max(s, axis=-1, keepdims=True))
        o = jnp.dot(p.astype(BF16), v_ref[0, :, sl].astype(BF16), preferred_element_type=F32)
        outs.append((o / jnp.sum(p, axis=-1, keepdims=True)).astype(BF16))
    return jnp.concatenate(outs, axis=1)


def _split_bf16(x):
    hi = x.astype(BF16)
    lo = (x - hi.astype(F32)).astype(BF16)
    return hi, lo


def _cross_ffn_body(x_ref, k_ref, v_ref, gc_ref, wq_ref, gq_ref, gk_ref, wo_ref, gf_ref, wr_ref,
                    x2_ref, fp_ref, lg_ref):
    x = x_ref[...]
    c = _rms(x, gc_ref[...]).astype(BF16)
    qx = jnp.dot(c, wq_ref[...], preferred_element_type=F32)
    ox = _cross_heads(qx, k_ref, v_ref, gq_ref[...], gk_ref[...])
    x2 = x + jnp.dot(ox, wo_ref[...], preferred_element_type=F32)
    x2_ref[...] = x2
    f = _rms(x2, gf_ref[...])
    half = f.shape[1] // 2
    fr = f.astype(BF16).astype(F32)
    lo_bits = pltpu.bitcast(fr[:, :half], jnp.uint32) >> 16
    hi_bits = pltpu.bitcast(fr[:, half:], jnp.uint32) & jnp.uint32(0xFFFF0000)
    fp_ref[...] = lo_bits | hi_bits
    f_hi, f_lo = _split_bf16(f)
    w_hi, w_lo = _split_bf16(wr_ref[...])
    lg_ref[...] = (jnp.dot(f_hi, w_hi, preferred_element_type=F32)
                   + jnp.dot(f_hi, w_lo, preferred_element_type=F32)
                   + jnp.dot(f_lo, w_hi, preferred_element_type=F32))


def cross_ffn_block(x, kx, vx, g_cross, wq, g_qx, g_kx, wo, g_ffn, w_router, tm=256):
    n, d = x.shape
    b, n_mem, xw = kx.shape
    e = w_router.shape[1]
    tm = min(tm, n // b)
    tiles_per_seq = n // b // tm
    row = lambda i: (i, 0)
    fixed = lambda i: (0, 0)
    mem = lambda i: (i // tiles_per_seq, 0, 0)
    return pl.pallas_call(
        _cross_ffn_body,
        grid=(n // tm,),
        in_specs=[
            pl.BlockSpec((tm, d), row),
            pl.BlockSpec((1, n_mem, xw), mem),
            pl.BlockSpec((1, n_mem, xw), mem),
            pl.BlockSpec((1, d), fixed),
            pl.BlockSpec((d, xw), fixed),
            pl.BlockSpec((1, HEAD_DIM), fixed),
            pl.BlockSpec((1, HEAD_DIM), fixed),
            pl.BlockSpec((xw, d), fixed),
            pl.BlockSpec((1, d), fixed),
            pl.BlockSpec((d, e), fixed),
        ],
        out_specs=[pl.BlockSpec((tm, d), row), pl.BlockSpec((tm, d // 2), row), pl.BlockSpec((tm, e), row)],
        out_shape=[jax.ShapeDtypeStruct((n, d), F32), jax.ShapeDtypeStruct((n, d // 2), jnp.uint32),
                   jax.ShapeDtypeStruct((n, e), F32)],
        compiler_params=_cparams("arbitrary"),
        name="cross_ffn_block",
    )(x, kx, vx, g_cross.reshape(1, d), wq, g_qx, g_kx, wo, g_ffn.reshape(1, d), w_router)


def _unpack_rows(p_ref, o_ref, rows=256):
    m, half = p_ref.shape
    rows = min(rows, m)
    for r0 in range(0, m, rows):
        p = p_ref[r0:r0 + rows, :]
        o_ref[r0:r0 + rows, :half] = pltpu.bitcast(p << 16, F32).astype(BF16)
        o_ref[r0:r0 + rows, half:] = pltpu.bitcast(p & jnp.uint32(0xFFFF0000), F32).astype(BF16)


def _select_top_cap(lg_ref, aff_ref, sel_ref, cs_ref, *, n, cap):
    e = lg_ref.shape[0]
    lg = lg_ref[...]
    ex = jnp.exp(lg - jnp.max(lg, axis=0, keepdims=True))
    aff = ex / jnp.sum(ex, axis=0, keepdims=True)
    aff_ref[...] = aff
    bits = pltpu.bitcast(aff, jnp.int32)

    def search(i, prefix):
        cand = prefix | (jnp.int32(1) << (30 - i))
        cnt = jnp.sum(jnp.where(bits >= cand, 1.0, 0.0), axis=1, keepdims=True)
        return jnp.where(cnt >= cap, cand, prefix)

    thr = lax.fori_loop(0, 31, search, jnp.zeros((e, 1), jnp.int32))
    gt = bits > thr
    need = cap - jnp.sum(jnp.where(gt, 1.0, 0.0), axis=1, keepdims=True)

    tri = (lax.broadcasted_iota(jnp.int32, (LANES, LANES), 0)
           <= lax.broadcasted_iota(jnp.int32, (LANES, LANES), 1)).astype(BF16)

    def cumsum_lanes(ref):
        def blk(j, carry):
            c0 = pl.multiple_of(j * LANES, LANES)
            w = jnp.dot(ref[:, pl.ds(c0, LANES)].astype(BF16), tri, preferred_element_type=F32) + carry
            ref[:, pl.ds(c0, LANES)] = w
            return w[:, LANES - 1:LANES]
        lax.fori_loop(0, n // LANES, blk, jnp.zeros((e, 1), F32))

    eq = bits == thr
    cs_ref[...] = jnp.where(eq, 1.0, 0.0)
    cumsum_lanes(cs_ref)
    sel = gt | (eq & (cs_ref[...] <= need))
    sel_ref[...] = jnp.where(sel, 1.0, 0.0)
    cs_ref[...] = sel_ref[...]
    cumsum_lanes(cs_ref)
    cs_ref[...] = cs_ref[...] * sel_ref[...]


def _route_body(lg_ref, idx_ref, gate_ref, aff_ref, sel_ref, cs_ref, aff_rows, cs_rows, *, n, cap, chunk):
    ei = pl.program_id(0)

    @pl.when(ei == 0)
    def _():
        _select_top_cap(lg_ref, aff_ref, sel_ref, cs_ref, n=n, cap=cap)
        for k in range(lg_ref.shape[0]):
            aff_rows[k] = aff_ref[k:k + 1, :]
            cs_rows[k] = cs_ref[k:k + 1, :]

    tok = lax.broadcasted_iota(jnp.int32, (1, n), 1).astype(F32)
    lane = lax.broadcasted_iota(jnp.int32, (chunk, LANES), 1).astype(F32)
    sub = lax.broadcasted_iota(jnp.int32, (chunk, 1), 0).astype(F32)

    def per_chunk(ci, carry):
        s0 = pl.multiple_of(ci * chunk, chunk)
        lo = s0.astype(F32)
        row = cs_rows[ei]
        inside = jnp.abs(row - (lo + 0.5 * (chunk + 1))) < 0.5 * chunk
        t_first = jnp.min(jnp.where(inside, tok, float(n)), axis=1, keepdims=True)[0, 0]
        t_last = jnp.max(jnp.where(inside, tok, -1.0), axis=1, keepdims=True)[0, 0]
        j_lo = t_first.astype(jnp.int32) // LANES
        j_hi = t_last.astype(jnp.int32) // LANES + 1
        want = sub + (lo + 1.0)

        def per_blk(j, accs):
            acc_i, acc_g = accs
            c0 = pl.multiple_of(j * LANES, LANES)
            hit = cs_rows[ei, :, pl.ds(c0, LANES)] == want
            acc_i = acc_i + jnp.where(hit, lane + c0.astype(F32), 0.0)
            acc_g = acc_g + jnp.where(hit, aff_rows[ei, :, pl.ds(c0, LANES)], 0.0)
            return acc_i, acc_g

        z = jnp.zeros((chunk, LANES), F32)
        acc_i, acc_g = lax.fori_loop(j_lo, j_hi, per_blk, (z, z))
        idx_ref[0, pl.ds(s0, chunk), :] = jnp.sum(acc_i, axis=1, keepdims=True).astype(jnp.int32)
        gate_ref[0, pl.ds(s0, chunk), :] = jnp.sum(acc_g, axis=1, keepdims=True)
        return carry

    lax.fori_loop(0, cap // chunk, per_chunk, 0)


def route(logits_t, cap):
    e, n = logits_t.shape
    chunk = min(LANES, cap)
    return pl.pallas_call(
        functools.partial(_route_body, n=n, cap=cap, chunk=chunk),
        grid=(e,),
        in_specs=[pl.BlockSpec((e, n), lambda i: (0, 0))],
        out_specs=[pl.BlockSpec((1, cap, 1), lambda i: (i, 0, 0)), pl.BlockSpec((1, cap, 1), lambda i: (i, 0, 0))],
        out_shape=[jax.ShapeDtypeStruct((e, cap, 1), jnp.int32), jax.ShapeDtypeStruct((e, cap, 1), F32)],
        scratch_shapes=[pltpu.VMEM((e, n), F32), pltpu.VMEM((e, n), F32), pltpu.VMEM((e, n), F32),
                        pltpu.VMEM((e, 1, n), F32), pltpu.VMEM((e, 1, n), F32)],
        compiler_params=_cparams("arbitrary"),
        name="route",
    )(logits_t)


def _row_copy(src_ref, dst_ref, src_row, dst_row, sem):
    return pltpu.make_async_copy(src_ref.at[pl.ds(src_row, 1)], dst_ref.at[pl.ds(dst_row, 1)], sem)


def _wait_rows(src_ref, dst_ref, n, sem):
    pltpu.make_async_copy(src_ref.at[pl.ds(0, n)], dst_ref.at[pl.ds(0, n)], sem).wait()


ROW_DMA_UNROLL = 8


def _gate_up_body(idx_ref, f_ref, wg_ref, wu_ref, o_ref, xg_ref, xb_ref, sems, *, tm, n_f, n_tiles):
    fi = pl.program_id(2)
    tile = pl.program_id(0) * pl.num_programs(1) + pl.program_id(1)
    slot = tile % 2
    share = tm // n_f

    def start_row(tile_id, slot_id, r):
        _row_copy(f_ref, xg_ref.at[slot_id], idx_ref[tile_id * tm + r], r, sems.at[slot_id]).start()

    @pl.when((tile == 0) & (fi == 0))
    def _():
        def issue(r, c):
            start_row(0, 0, r)
            return c
        lax.fori_loop(0, tm, issue, 0, unroll=ROW_DMA_UNROLL)

    @pl.when(fi == 0)
    def _():
        _wait_rows(f_ref, xg_ref.at[slot], tm, sems.at[slot])
        _unpack_rows(xg_ref.at[slot], xb_ref)

    nxt = (tile + 1) % n_tiles
    for r in range(share):
        start_row(nxt, 1 - slot, fi * share + r)

    x = xb_ref[...]
    g = jnp.dot(x, wg_ref[0].astype(BF16), preferred_element_type=F32)
    u = jnp.dot(x, wu_ref[0].astype(BF16), preferred_element_type=F32)
    o_ref[0] = (g * (1.0 / (1.0 + jnp.exp(-g))) * u).astype(o_ref.dtype)

    @pl.when((tile == n_tiles - 1) & (fi == n_f - 1))
    def _():
        _wait_rows(f_ref, xg_ref.at[1 - slot], tm, sems.at[1 - slot])


def expert_gate_up(f_packed, idx, e, w_gate, w_up, tm=1024, tf=256):
    c = idx.shape[0] // e
    dh = f_packed.shape[1]
    d = 2 * dh
    f = w_gate.shape[2]
    tm, tf = min(tm, c), min(tf, f)
    n_f = f // tf
    assert c % tm == 0 and tm % n_f == 0
    return pl.pallas_call(
        functools.partial(_gate_up_body, tm=tm, n_f=n_f, n_tiles=e * (c // tm)),
        grid_spec=pltpu.PrefetchScalarGridSpec(
            num_scalar_prefetch=1,
            grid=(e, c // tm, n_f),
            in_specs=[
                pl.BlockSpec(memory_space=pl.ANY),
                pl.BlockSpec((1, d, tf), lambda ei, mi, fi, idx_ref: (ei, 0, fi)),
                pl.BlockSpec((1, d, tf), lambda ei, mi, fi, idx_ref: (ei, 0, fi)),
            ],
            out_specs=pl.BlockSpec((1, tm, tf), lambda ei, mi, fi, idx_ref: (ei, mi, fi)),
            scratch_shapes=[pltpu.VMEM((2, tm, dh), jnp.uint32), pltpu.VMEM((tm, d), BF16),
                            pltpu.SemaphoreType.DMA((2,))],
        ),
        out_shape=jax.ShapeDtypeStruct((e, c, f), BF16),
        compiler_params=_cparams("arbitrary", "arbitrary", "arbitrary"),
        name="expert_gate_up",
    )(idx, f_packed, w_gate, w_up)


def _down_body(h_ref, w_ref, gate_ref, o_ref):
    y = jnp.dot(h_ref[0], w_ref[0].astype(BF16), preferred_element_type=F32) * gate_ref[0]
    o_ref[0] = y.astype(o_ref.dtype)


def expert_down(hid, w_down, gates, tm=1024, tn=512):
    e, c, f = hid.shape
    d = w_down.shape[2]
    tm, tn = min(tm, c), min(tn, d)
    return pl.pallas_call(
        _down_body,
        grid=(e, c // tm, d // tn),
        in_specs=[
            pl.BlockSpec((1, tm, f), lambda ei, mi, ni: (ei, mi, 0)),
            pl.BlockSpec((1, f, tn), lambda ei, mi, ni: (ei, 0, ni)),
            pl.BlockSpec((1, tm, 1), lambda ei, mi, ni: (ei, mi, 0)),
        ],
        out_specs=pl.BlockSpec((1, tm, tn), lambda ei, mi, ni: (ei, mi, ni)),
        out_shape=jax.ShapeDtypeStruct((e, c, d), BF16),
        compiler_params=_cparams("arbitrary", "arbitrary", "arbitrary"),
        name="expert_down",
    )(hid, w_down, gates)


COMBINE_SLOTS = 3


def _combine_body(idx_ref, ye_ref, x_ref, o_ref, buf_ref, sem_in, sem_out, *, rows, tpe, n_tiles):
    del x_ref
    i = pl.program_id(0)
    slot = i % COMBINE_SLOTS
    first = (i % tpe) == 0
    last = (i % tpe) == tpe - 1

    def fetch(tile, s):
        def one(r, c):
            _row_copy(o_ref, buf_ref.at[s], idx_ref[tile * rows + r], r, sem_in.at[s]).start()
            return c
        lax.fori_loop(0, rows, one, 0, unroll=ROW_DMA_UNROLL)

    def put(tile, s):
        def one(r, c):
            _row_copy(buf_ref.at[s], o_ref, r, idx_ref[tile * rows + r], sem_out.at[s]).start()
            return c
        lax.fori_loop(0, rows, one, 0, unroll=ROW_DMA_UNROLL)

    def wait_put(s):
        _wait_rows(buf_ref.at[s], o_ref, rows, sem_out.at[s])

    @pl.when((i >= 2) & (((i - 1) % tpe) != 0))
    def _():
        wait_put((i - 2) % COMBINE_SLOTS)

    @pl.when(first)
    def _():
        @pl.when(i >= 1)
        def _():
            wait_put((i - 1) % COMBINE_SLOTS)
        fetch(i, slot)

    @pl.when(jnp.logical_not(last))
    def _():
        fetch(i + 1, (i + 1) % COMBINE_SLOTS)

    _wait_rows(o_ref, buf_ref.at[slot], rows, sem_in.at[slot])
    buf_ref[slot] = buf_ref[slot] + ye_ref[...].astype(F32)
    put(i, slot)

    @pl.when(i == n_tiles - 1)
    def _():
        if n_tiles >= 2 and (n_tiles - 1) % tpe != 0:
            wait_put((n_tiles - 2) % COMBINE_SLOTS)
        wait_put((n_tiles - 1) % COMBINE_SLOTS)


def combine(x, ye, idx, rows, rows_per_expert):
    r, d = ye.shape
    assert rows_per_expert % rows == 0
    return pl.pallas_call(
        functools.partial(_combine_body, rows=rows, tpe=rows_per_expert // rows, n_tiles=r // rows),
        grid_spec=pltpu.PrefetchScalarGridSpec(
            num_scalar_prefetch=1,
            grid=(r // rows,),
            in_specs=[pl.BlockSpec((rows, d), lambda i, idx_ref: (i, 0)), pl.BlockSpec(memory_space=pl.ANY)],
            out_specs=pl.BlockSpec(memory_space=pl.ANY),
            scratch_shapes=[pltpu.VMEM((COMBINE_SLOTS, rows, d), F32), pltpu.SemaphoreType.DMA((COMBINE_SLOTS,)),
                            pltpu.SemaphoreType.DMA((COMBINE_SLOTS,))],
        ),
        out_shape=jax.ShapeDtypeStruct(x.shape, x.dtype),
        input_output_aliases={2: 0},
        compiler_params=_cparams("arbitrary"),
        name="combine",
    )(idx, ye, x)


def _rel_bucket(rel):
    nb = REL_BUCKETS // 2
    max_exact = nb // 2
    ret = jnp.where(rel > 0, nb, 0)
    n = jnp.abs(rel)
    nf = jnp.maximum(n, 1).astype(F32)
    large = max_exact + (jnp.log(nf / max_exact) / math.log(REL_MAX_DIST / max_exact)
                         * (nb - max_exact)).astype(jnp.int32)
    large = jnp.minimum(large, nb - 1)
    return ret + jnp.where(n < max_exact, n, large)


def _bucket_tiles(t):
    r = np.arange(t)[:, None]
    c = np.arange(t)[None, :]
    rel = np.stack([dt * t + (c - r) for dt in (-2, -1, 0, 1, 2)]).astype(np.int32)
    return _rel_bucket(jnp.asarray(rel)).astype(jnp.int32)


def _axial_rope_tables(s):
    rows = s // GRID_W
    row = jnp.repeat(jnp.arange(rows), GRID_W).astype(F32)
    col = jnp.tile(jnp.arange(GRID_W), rows).astype(F32)
    half = HEAD_DIM // 2
    inv = 1.0 / (ROPE_THETA ** (jnp.arange(0, half, 2, dtype=F32) / half))
    ang_r = row[:, None] * inv
    ang_c = col[:, None] * inv
    ang = jnp.concatenate([ang_r, ang_r, ang_c, ang_c], axis=-1)
    sign = np.where((np.arange(HEAD_DIM) % half) < half // 2, -1.0, 1.0).astype(np.float32)
    return jnp.cos(ang), jnp.sin(ang) * sign


def _layer(x, mem, rel_table, g_mix, w_in, g_qa, g_ka, lam_vecs, g_subln, g_qb, g_kb, w_o, g_cross,
           g_mem, wq_x, wk_x, wv_x, g_qx, g_kx, wo_x, g_ffn, w_router, w_gate, w_up, w_down, lam_init):
    b, s, dm = x.shape
    n = b * s
    n_mem = mem.shape[1]
    x2d = x.reshape(n, dm)
    row = lambda v: v.reshape(1, -1)

    h = rmsnorm_cast(x2d, g_mix)
    proj = matmul([h], w_in, out_dtype=BF16).reshape(b, s, -1)
    t_a = min(256, s)
    assert s % t_a == 0 and t_a >= REL_MAX_DIST, "bias tiles need saturated buckets two tiles away"
    bounded_a = _bounded_flag(_score_bound(g_qa, g_ka, jnp.max(jnp.abs(rel_table))))
    oa = diff_attention(proj, bounded_a, rel_table, _bucket_tiles(t_a), row(g_qa), row(g_ka), row(g_subln),
                        lam_vecs, lam_init, t_a)
    cos, sin_signed = _axial_rope_tables(s)
    ob = gqa_attention(proj, _bounded_flag(_score_bound(g_qb, g_kb)), cos, sin_signed, row(g_qb), row(g_kb))
    x1 = matmul([oa.reshape(n, -1), ob.reshape(n, -1)], w_o, res=x2d, tm=512)

    m = rmsnorm_cast(mem.reshape(b * n_mem, dm), g_mem)
    kx = matmul([m], wk_x).reshape(b, n_mem, -1)
    vx = matmul([m], wv_x).reshape(b, n_mem, -1)
    x2, f_packed, logits = cross_ffn_block(x1, kx, vx, g_cross, wq_x.astype(BF16), row(g_qx), row(g_kx),
                                           wo_x.astype(BF16), g_ffn, w_router)

    e = w_router.shape[1]
    cap = EC_FACTOR * n // e
    idx, gates = route(logits.T, cap)
    idx_flat = idx.reshape(e * cap)
    hid = expert_gate_up(f_packed, idx_flat, e, w_gate, w_up)
    ye = expert_down(hid, w_down, gates)
    out = combine(x2, ye.reshape(e * cap, dm), idx_flat, rows=min(256, cap), rows_per_expert=cap)
    return out.reshape(b, s, dm)


def kernel(x_prompt, x_sample, mem_prompt, mem_sample, rel_table, g_mix, w_in, g_qa, g_ka, lam_q1, lam_k1, lam_q2, lam_k2, g_subln, g_qb, g_kb, w_o, g_cross, g_mem, wq_x, wk_x, wv_x, g_qx, g_kx, wo_x, g_ffn, w_router, w_gate, w_up, w_down):
    outs = []
    for x, mem in ((x_prompt, mem_prompt), (x_sample, mem_sample)):
        for l in range(g_mix.shape[0]):
            lam_init = 0.8 - 0.6 * math.exp(-0.3 * l)
            lam_vecs = jnp.stack([lam_q1[l], lam_k1[l], lam_q2[l], lam_k2[l]])
            x = _layer(x, mem, rel_table, g_mix[l], w_in[l], g_qa[l], g_ka[l], lam_vecs, g_subln[l], g_qb[l],
                       g_kb[l], w_o[l], g_cross[l], g_mem[l], wq_x[l], wk_x[l], wv_x[l], g_qx[l], g_kx[l],
                       wo_x[l], g_ffn[l], w_router[l], w_gate[l], w_up[l], w_down[l], lam_init)
        outs.append(x)
    return tuple(outs)
```

```python
import functools
import math

import numpy as np
import jax
import jax.numpy as jnp
from jax import lax
from jax.experimental import pallas as pl
from jax.experimental.pallas import tpu as pltpu

F32 = jnp.float32
BF16 = jnp.bfloat16

HEAD_DIM = 128
N_HEADS_A = 8
N_HEADS_B = 16
N_KV_B = 4
N_HEADS_X = 4
EC_FACTOR = 2
GRID_W = 64
REL_BUCKETS = 32
REL_MAX_DIST = 128
ROPE_THETA = 10000.0
EPS = 1e-6
ATTN_SCALE = HEAD_DIM ** -0.5
LOG2E = math.log2(math.e)

V7X_VMEM_BYTES = 64 * 1024 * 1024
VMEM_LIMIT = V7X_VMEM_BYTES - 8 * 1024 * 1024
LANES = 128
SUBLANES = 8

SAFE_SCORE_BOUND = 30.0
BF16_NORM_SLACK = (1.0 + 2.0 ** -8) ** 2
SCORE_MATMUL_KEYS = 1024
SCORE_MATMULS_PER_TRIP = 4
DIFF_QUERY_TILES_PER_STEP = 4

NT_DIMS = (((1,), (1,)), ((), ()))


def _cparams(*sem):
    return pltpu.CompilerParams(dimension_semantics=sem, vmem_limit_bytes=VMEM_LIMIT)


def _rms(x, g):
    x = x.astype(F32)
    return x * lax.rsqrt(jnp.mean(x * x, axis=-1, keepdims=True) + EPS) * g


def _smem_spec():
    return pl.BlockSpec(memory_space=pltpu.SMEM)


def _rmsnorm_body(x_ref, g_ref, o_ref):
    o_ref[...] = _rms(x_ref[...], g_ref[...]).astype(o_ref.dtype)


def rmsnorm_cast(x, g, tm=256):
    m, d = x.shape
    tm = min(tm, m)
    return pl.pallas_call(
        _rmsnorm_body,
        grid=(m // tm,),
        in_specs=[pl.BlockSpec((tm, d), lambda i: (i, 0)), pl.BlockSpec((1, d), lambda i: (0, 0))],
        out_specs=pl.BlockSpec((tm, d), lambda i: (i, 0)),
        out_shape=jax.ShapeDtypeStruct((m, d), BF16),
        compiler_params=_cparams("arbitrary"),
        name="rmsnorm_cast",
    )(x, g.reshape(1, d))


def _mm_body(*refs, n_a, has_res):
    a_refs, w_ref = refs[:n_a], refs[n_a]
    r_ref = refs[n_a + 1] if has_res else None
    o_ref, wb_ref = refs[-2:]

    @pl.when(pl.program_id(1) == 0)
    def _():
        wb_ref[...] = w_ref[...].astype(BF16)

    acc, k0 = None, 0
    for a_ref in a_refs:
        k = a_ref.shape[1]
        part = jnp.dot(a_ref[...], wb_ref[k0:k0 + k, :], preferred_element_type=F32)
        acc = part if acc is None else acc + part
        k0 += k
    if has_res:
        acc = acc + r_ref[...]
    o_ref[...] = acc.astype(o_ref.dtype)


def matmul(a_list, w, res=None, out_dtype=F32, tm=1024, tn=1024):
    m = a_list[0].shape[0]
    k, n = w.shape
    assert sum(a.shape[1] for a in a_list) == k
    tm, tn = min(tm, m), min(tn, n)
    in_specs = [pl.BlockSpec((tm, a.shape[1]), lambda j, i: (i, 0)) for a in a_list]
    in_specs.append(pl.BlockSpec((k, tn), lambda j, i: (0, j), pipeline_mode=pl.Buffered(1)))
    args = list(a_list) + [w]
    if res is not None:
        in_specs.append(pl.BlockSpec((tm, tn), lambda j, i: (i, j)))
        args.append(res)
    return pl.pallas_call(
        functools.partial(_mm_body, n_a=len(a_list), has_res=res is not None),
        grid=(n // tn, m // tm),
        in_specs=in_specs,
        out_specs=pl.BlockSpec((tm, tn), lambda j, i: (i, j)),
        out_shape=jax.ShapeDtypeStruct((m, n), out_dtype),
        scratch_shapes=[pltpu.VMEM((k, tn), BF16)],
        compiler_params=_cparams("arbitrary", "arbitrary"),
        name="matmul",
    )(*args)


def _online_softmax_step(s, v, m_ref, l_ref, acc_ref):
    m_old = m_ref[...]
    m_new = jnp.maximum(m_old, jnp.max(s, axis=-1, keepdims=True))
    alpha = jnp.exp2(m_old - m_new)
    p = jnp.exp2(s - m_new)
    l_ref[...] = alpha * l_ref[...] + jnp.sum(p, axis=-1, keepdims=True)
    acc_ref[...] = alpha * acc_ref[...] + jnp.dot(p.astype(BF16), v, preferred_element_type=F32)
    m_ref[...] = m_new


def _reset_online(m_ref, l_ref, acc_ref):
    m_ref[...] = jnp.full(m_ref.shape, -jnp.inf, F32)
    l_ref[...] = jnp.zeros(l_ref.shape, F32)
    acc_ref[...] = jnp.zeros(acc_ref.shape, F32)


def _rope(x, cos, sin_signed):
    lane = lax.broadcasted_iota(jnp.int32, x.shape, 1)
    first = (lane % (HEAD_DIM // 2)) < (HEAD_DIM // 4)
    xr = jnp.where(first, pltpu.roll(x, HEAD_DIM - HEAD_DIM // 4, 1), pltpu.roll(x, HEAD_DIM // 4, 1))
    return x * cos + xr * sin_signed


def _score_bound(g_q, g_k, extra=0.0):
    return HEAD_DIM * ATTN_SCALE * BF16_NORM_SLACK * jnp.max(jnp.abs(g_q)) * jnp.max(jnp.abs(g_k)) + extra


def _bounded_flag(bound):
    return (bound <= SAFE_SCORE_BOUND).astype(jnp.int32).reshape(1)


def _diff_attn_body(flag_ref, tab_ref, q_ref, k_ref, v_ref, bkt_ref, gq_ref, gk_ref, gs_ref, lam_ref, o_ref,
                    kt_ref, vb_ref, bias_ref, q_scr, p_ref, lsum_ref, m_ref, l_ref, acc_ref,
                    *, t, sub, cw, n_kt, group, lam_init):
    hi = pl.program_id(1)
    qi = pl.program_id(2)
    d = HEAD_DIM

    @pl.when(qi == 0)
    def _():
        for j in range(2):
            kj = k_ref[0, :, j * d:(j + 1) * d]
            kt_ref[j] = _rms(kj, gk_ref[...]).T.astype(BF16)
        vb_ref[...] = v_ref[0].astype(BF16)

        def build(ti, c):
            bk = bkt_ref[ti]
            acc = jnp.zeros((t, t), F32)
            for b in range(REL_BUCKETS):
                acc = jnp.where(bk == b, tab_ref[b, hi], acc)
            bias_ref[ti] = acc * LOG2E
            return c
        lax.fori_loop(0, 5, build, 0)

    for j in range(2):
        qj = q_ref[0, :, j * d:(j + 1) * d]
        q_scr[j] = (_rms(qj, gq_ref[...]) * (ATTN_SCALE * LOG2E)).astype(BF16)

    def finish(o1, o2):
        lam_v = lam_ref[...]
        lam = (jnp.exp(jnp.sum(lam_v[0:1] * lam_v[1:2], axis=-1, keepdims=True))
               - jnp.exp(jnp.sum(lam_v[2:3] * lam_v[3:4], axis=-1, keepdims=True)) + lam_init)
        o = o1 - lam * o2
        o_ref[0] = (_rms(o, gs_ref[...]) * (1.0 - lam_init)).astype(o_ref.dtype)

    kw = cw * t

    def block_scores(kb, j):
        k0 = pl.multiple_of(kb * kw, kw)
        s = jnp.dot(q_scr[j], kt_ref[j, :, pl.ds(k0, kw)], preferred_element_type=F32)
        rows = []
        for u in range(sub):
            cols = [s[u * t:(u + 1) * t, v * t:(v + 1) * t]
                    + bias_ref[jnp.clip(kb * cw + v - (qi * sub + u), -2, 2) + 2] for v in range(cw)]
            rows.append(cols[0] if cw == 1 else jnp.concatenate(cols, axis=1))
        return rows[0] if sub == 1 else jnp.concatenate(rows, axis=0)

    @pl.when(flag_ref[0] == 1)
    def _():
        lsum_ref[...] = jnp.zeros(lsum_ref.shape, F32)

        def step(kg, c):
            parts = [None, None]
            for u in range(group):
                kb = kg * group + u
                k0 = pl.multiple_of(kb * kw, kw)
                for j in range(2):
                    p = jnp.exp2(block_scores(kb, j))
                    p_ref[j, :, pl.ds(k0, kw)] = p.astype(BF16)
                    for c0 in range(0, kw, LANES):
                        pc = p[:, c0:c0 + LANES]
                        parts[j] = pc if parts[j] is None else parts[j] + pc
            for j in range(2):
                lsum_ref[j] += parts[j]
            return c
        lax.fori_loop(0, n_kt // (cw * group), step, 0)
        outs = []
        for j in range(2):
            l = jnp.sum(lsum_ref[j], axis=-1, keepdims=True)
            outs.append(jnp.dot(p_ref[j], vb_ref[...], preferred_element_type=F32) / l)
        finish(outs[0], outs[1])

    @pl.when(flag_ref[0] != 1)
    def _():
        _reset_online(m_ref, l_ref, acc_ref)

        def step(kb, c):
            k0 = pl.multiple_of(kb * kw, kw)
            vt = vb_ref[pl.ds(k0, kw), :]
            for j in range(2):
                _online_softmax_step(block_scores(kb, j), vt, m_ref.at[j], l_ref.at[j], acc_ref.at[j])
            return c
        lax.fori_loop(0, n_kt // cw, step, 0)
        finish(acc_ref[0] / l_ref[0], acc_ref[1] / l_ref[1])


def diff_attention(proj, bounded, rel_table, bucket_tiles, g_qa, g_ka, g_subln, lam_vecs, lam_init, t):
    b, s, _ = proj.shape
    h = N_HEADS_A
    w = 2 * HEAD_DIM
    n_kt = s // t
    cw = math.gcd(n_kt, SCORE_MATMUL_KEYS // t)
    group = math.gcd(n_kt // cw, SCORE_MATMULS_PER_TRIP)
    sub = math.gcd(n_kt, DIFF_QUERY_TILES_PER_STEP)
    tq = sub * t
    return pl.pallas_call(
        functools.partial(_diff_attn_body, t=t, sub=sub, cw=cw, n_kt=n_kt, group=group, lam_init=lam_init),
        grid=(b, h, s // tq),
        in_specs=[
            _smem_spec(),
            _smem_spec(),
            pl.BlockSpec((1, tq, w), lambda bi, hi, qi: (bi, qi, hi)),
            pl.BlockSpec((1, s, w), lambda bi, hi, qi: (bi, 0, h + hi)),
            pl.BlockSpec((1, s, w), lambda bi, hi, qi: (bi, 0, 2 * h + hi)),
            pl.BlockSpec((5, t, t), lambda bi, hi, qi: (0, 0, 0)),
            pl.BlockSpec((1, HEAD_DIM), lambda bi, hi, qi: (0, 0)),
            pl.BlockSpec((1, HEAD_DIM), lambda bi, hi, qi: (0, 0)),
            pl.BlockSpec((1, w), lambda bi, hi, qi: (0, 0)),
            pl.BlockSpec((4, HEAD_DIM), lambda bi, hi, qi: (0, 0)),
        ],
        out_specs=pl.BlockSpec((1, tq, w), lambda bi, hi, qi: (bi, qi, hi)),
        out_shape=jax.ShapeDtypeStruct((b, s, h * w), BF16),
        scratch_shapes=[
            pltpu.VMEM((2, HEAD_DIM, s), BF16), pltpu.VMEM((s, w), BF16), pltpu.VMEM((5, t, t), F32),
            pltpu.VMEM((2, tq, HEAD_DIM), BF16), pltpu.VMEM((2, tq, s), BF16), pltpu.VMEM((2, tq, LANES), F32),
            pltpu.VMEM((2, tq, 1), F32), pltpu.VMEM((2, tq, 1), F32), pltpu.VMEM((2, tq, w), F32),
        ],
        compiler_params=_cparams("arbitrary", "arbitrary", "arbitrary"),
        name="diff_attention",
    )(bounded, rel_table, proj, proj, proj, bucket_tiles, g_qa, g_ka, g_subln, lam_vecs)


def _gqa_body(flag_ref, q_ref, k_ref, v_ref, cq_ref, sq_ref, ck_ref, sk_ref, gq_ref, gk_ref, o_ref,
              kt_ref, va_ref, q_scr, p_ref, m_ref, l_ref, acc_ref, *, tq, tk, n_kt, group, rep):
    qi = pl.program_id(2)
    d = HEAD_DIM

    @pl.when(qi == 0)
    def _():
        kn = _rope(_rms(k_ref[0], gk_ref[...]), ck_ref[...], sk_ref[...])
        kt_ref[...] = kn.T.astype(BF16)
        va_ref[:, 0:d] = v_ref[0].astype(BF16)
        lane = lax.broadcasted_iota(jnp.int32, (va_ref.shape[0], d), 1)
        va_ref[:, d:2 * d] = jnp.where(lane == 0, 1.0, 0.0).astype(BF16)

    cq, sq = cq_ref[...], sq_ref[...]
    for r in range(rep):
        qr = _rope(_rms(q_ref[0, :, r * d:(r + 1) * d], gq_ref[...]), cq, sq)
        q_scr[r * tq:(r + 1) * tq, :] = (qr * (ATTN_SCALE * LOG2E)).astype(BF16)

    def tile_scores(kt):
        k0 = pl.multiple_of(kt * tk, tk)
        return jnp.dot(q_scr[...], kt_ref[:, pl.ds(k0, tk)], preferred_element_type=F32)

    def write(o):
        for r in range(rep):
            o_ref[0, :, r * d:(r + 1) * d] = o[r * tq:(r + 1) * tq].astype(o_ref.dtype)

    @pl.when(flag_ref[0] == 1)
    def _():
        def step(kg, c):
            for u in range(group):
                kt = kg * group + u
                k0 = pl.multiple_of(kt * tk, tk)
                p_ref[:, pl.ds(k0, tk)] = jnp.exp2(tile_scores(kt)).astype(BF16)
            return c
        lax.fori_loop(0, n_kt // group, step, 0)
        o = jnp.dot(p_ref[...], va_ref[...], preferred_element_type=F32)
        write(o[:, 0:d] / o[:, d:d + 1])

    @pl.when(flag_ref[0] != 1)
    def _():
        _reset_online(m_ref, l_ref, acc_ref)

        def step(kt, c):
            k0 = pl.multiple_of(kt * tk, tk)
            _online_softmax_step(tile_scores(kt), va_ref[pl.ds(k0, tk), 0:d], m_ref, l_ref, acc_ref)
            return c
        lax.fori_loop(0, n_kt, step, 0)
        write(acc_ref[...] / l_ref[...])


def gqa_attention(proj, bounded, cos, sin_signed, g_qb, g_kb, tq=512, tk=SCORE_MATMUL_KEYS):
    b, s, _ = proj.shape
    g = N_KV_B
    rep = N_HEADS_B // N_KV_B
    d = HEAD_DIM
    qw = rep * d
    tq, tk = min(tq, s), min(tk, s)
    a_w = N_HEADS_A * 2 * d
    q_blk0 = 3 * a_w // qw
    k_blk0 = (3 * a_w + N_HEADS_B * d) // d
    v_blk0 = k_blk0 + g
    rows = rep * tq
    return pl.pallas_call(
        functools.partial(_gqa_body, tq=tq, tk=tk, n_kt=s // tk,
                          group=math.gcd(s // tk, SCORE_MATMULS_PER_TRIP), rep=rep),
        grid=(b, g, s // tq),
        in_specs=[
            _smem_spec(),
            pl.BlockSpec((1, tq, qw), lambda bi, gi, qi: (bi, qi, q_blk0 + gi)),
            pl.BlockSpec((1, s, d), lambda bi, gi, qi: (bi, 0, k_blk0 + gi)),
            pl.BlockSpec((1, s, d), lambda bi, gi, qi: (bi, 0, v_blk0 + gi)),
            pl.BlockSpec((tq, d), lambda bi, gi, qi: (qi, 0)),
            pl.BlockSpec((tq, d), lambda bi, gi, qi: (qi, 0)),
            pl.BlockSpec((s, d), lambda bi, gi, qi: (0, 0)),
            pl.BlockSpec((s, d), lambda bi, gi, qi: (0, 0)),
            pl.BlockSpec((1, d), lambda bi, gi, qi: (0, 0)),
            pl.BlockSpec((1, d), lambda bi, gi, qi: (0, 0)),
        ],
        out_specs=pl.BlockSpec((1, tq, qw), lambda bi, gi, qi: (bi, qi, gi)),
        out_shape=jax.ShapeDtypeStruct((b, s, g * qw), BF16),
        scratch_shapes=[
            pltpu.VMEM((d, s), BF16), pltpu.VMEM((s, 2 * d), BF16),
            pltpu.VMEM((rows, d), BF16), pltpu.VMEM((rows, s), BF16),
            pltpu.VMEM((rows, 1), F32), pltpu.VMEM((rows, 1), F32), pltpu.VMEM((rows, d), F32),
        ],
        compiler_params=_cparams("arbitrary", "arbitrary", "arbitrary"),
        name="gqa_attention",
    )(bounded, proj, proj, proj, cos, sin_signed, cos, sin_signed, g_qb, g_kb)


def _cross_heads(qx, k_ref, v_ref, gq, gk):
    d = HEAD_DIM
    outs = []
    for hh in range(N_HEADS_X):
        sl = slice(hh * d, (hh + 1) * d)
        qn = (_rms(qx[:, sl], gq) * ATTN_SCALE).astype(BF16)
        kn = _rms(k_ref[0, :, sl], gk).astype(BF16)
        s = lax.dot_general(qn, kn, NT_DIMS, preferred_element_type=F32)
        p = jnp.exp(s - jnp.max(s, axis=-1, keepdims=True))
        o = jnp.dot(p.astype(BF16), v_ref[0, :, sl].astype(BF16), preferred_element_type=F32)
        outs.append((o / jnp.sum(p, axis=-1, keepdims=True)).astype(BF16))
    return jnp.concatenate(outs, axis=1)


def _split_bf16(x):
    hi = x.astype(BF16)
    lo = (x - hi.astype(F32)).astype(BF16)
    return hi, lo


def _cross_ffn_body(x_ref, k_ref, v_ref, gc_ref, wq_ref, gq_ref, gk_ref, wo_ref, gf_ref, wr_ref,
                    x2_ref, fp_ref, lg_ref):
    x = x_ref[...]
    c = _rms(x, gc_ref[...]).astype(BF16)
    qx = jnp.dot(c, wq_ref[...], preferred_element_type=F32)
    ox = _cross_heads(qx, k_ref, v_ref, gq_ref[...], gk_ref[...])
    x2 = x + jnp.dot(ox, wo_ref[...], preferred_element_type=F32)
    x2_ref[...] = x2
    f = _rms(x2, gf_ref[...])
    half = f.shape[1] // 2
    fr = f.astype(BF16).astype(F32)
    lo_bits = pltpu.bitcast(fr[:, :half], jnp.uint32) >> 16
    hi_bits = pltpu.bitcast(fr[:, half:], jnp.uint32) & jnp.uint32(0xFFFF0000)
    fp_ref[...] = lo_bits | hi_bits
    f_hi, f_lo = _split_bf16(f)
    w_hi, w_lo = _split_bf16(wr_ref[...])
    lg_ref[...] = (jnp.dot(f_hi, w_hi, preferred_element_type=F32)
                   + jnp.dot(f_hi, w_lo, preferred_element_type=F32)
                   + jnp.dot(f_lo, w_hi, preferred_element_type=F32))


def cross_ffn_block(x, kx, vx, g_cross, wq, g_qx, g_kx, wo, g_ffn, w_router, tm=256):
    n, d = x.shape
    b, n_mem, xw = kx.shape
    e = w_router.shape[1]
    tm = min(tm, n // b)
    tiles_per_seq = n // b // tm
    row = lambda i: (i, 0)
    fixed = lambda i: (0, 0)
    mem = lambda i: (i // tiles_per_seq, 0, 0)
    return pl.pallas_call(
        _cross_ffn_body,
        grid=(n // tm,),
        in_specs=[
            pl.BlockSpec((tm, d), row),
            pl.BlockSpec((1, n_mem, xw), mem),
            pl.BlockSpec((1, n_mem, xw), mem),
            pl.BlockSpec((1, d), fixed),
            pl.BlockSpec((d, xw), fixed),
            pl.BlockSpec((1, HEAD_DIM), fixed),
            pl.BlockSpec((1, HEAD_DIM), fixed),
            pl.BlockSpec((xw, d), fixed),
            pl.BlockSpec((1, d), fixed),
            pl.BlockSpec((d, e), fixed),
        ],
        out_specs=[pl.BlockSpec((tm, d), row), pl.BlockSpec((tm, d // 2), row), pl.BlockSpec((tm, e), row)],
        out_shape=[jax.ShapeDtypeStruct((n, d), F32), jax.ShapeDtypeStruct((n, d // 2), jnp.uint32),
                   jax.ShapeDtypeStruct((n, e), F32)],
        compiler_params=_cparams("arbitrary"),
        name="cross_ffn_block",
    )(x, kx, vx, g_cross.reshape(1, d), wq, g_qx, g_kx, wo, g_ffn.reshape(1, d), w_router)


def _unpack_rows(p_ref, o_ref, rows=256):
    m, half = p_ref.shape
    rows = min(rows, m)
    for r0 in range(0, m, rows):
        p = p_ref[r0:r0 + rows, :]
        o_ref[r0:r0 + rows, :half] = pltpu.bitcast(p << 16, F32).astype(BF16)
        o_ref[r0:r0 + rows, half:] = pltpu.bitcast(p & jnp.uint32(0xFFFF0000), F32).astype(BF16)


def _select_top_cap(lg_ref, aff_ref, sel_ref, cs_ref, *, n, cap):
    e = lg_ref.shape[0]
    lg = lg_ref[...]
    ex = jnp.exp(lg - jnp.max(lg, axis=0, keepdims=True))
    aff = ex / jnp.sum(ex, axis=0, keepdims=True)
    aff_ref[...] = aff
    bits = pltpu.bitcast(aff, jnp.int32)

    def search(i, prefix):
        cand = prefix | (jnp.int32(1) << (30 - i))
        cnt = jnp.sum(jnp.where(bits >= cand, 1.0, 0.0), axis=1, keepdims=True)
        return jnp.where(cnt >= cap, cand, prefix)

    thr = lax.fori_loop(0, 31, search, jnp.zeros((e, 1), jnp.int32))
    gt = bits > thr
    need = cap - jnp.sum(jnp.where(gt, 1.0, 0.0), axis=1, keepdims=True)

    tri = (lax.broadcasted_iota(jnp.int32, (LANES, LANES), 0)
           <= lax.broadcasted_iota(jnp.int32, (LANES, LANES), 1)).astype(BF16)

    def cumsum_lanes(ref):
        def blk(j, carry):
            c0 = pl.multiple_of(j * LANES, LANES)
            w = jnp.dot(ref[:, pl.ds(c0, LANES)].astype(BF16), tri, preferred_element_type=F32) + carry
            ref[:, pl.ds(c0, LANES)] = w
            return w[:, LANES - 1:LANES]
        lax.fori_loop(0, n // LANES, blk, jnp.zeros((e, 1), F32))

    eq = bits == thr
    cs_ref[...] = jnp.where(eq, 1.0, 0.0)
    cumsum_lanes(cs_ref)
    sel = gt | (eq & (cs_ref[...] <= need))
    sel_ref[...] = jnp.where(sel, 1.0, 0.0)
    cs_ref[...] = sel_ref[...]
    cumsum_lanes(cs_ref)
    cs_ref[...] = cs_ref[...] * sel_ref[...]


def _route_body(lg_ref, idx_ref, gate_ref, aff_ref, sel_ref, cs_ref, aff_rows, cs_rows, *, n, cap, chunk):
    ei = pl.program_id(0)

    @pl.when(ei == 0)
    def _():
        _select_top_cap(lg_ref, aff_ref, sel_ref, cs_ref, n=n, cap=cap)
        for k in range(lg_ref.shape[0]):
            aff_rows[k] = aff_ref[k:k + 1, :]
            cs_rows[k] = cs_ref[k:k + 1, :]

    tok = lax.broadcasted_iota(jnp.int32, (1, n), 1).astype(F32)
    lane = lax.broadcasted_iota(jnp.int32, (chunk, LANES), 1).astype(F32)
    sub = lax.broadcasted_iota(jnp.int32, (chunk, 1), 0).astype(F32)

    def per_chunk(ci, carry):
        s0 = pl.multiple_of(ci * chunk, chunk)
        lo = s0.astype(F32)
        row = cs_rows[ei]
        inside = jnp.abs(row - (lo + 0.5 * (chunk + 1))) < 0.5 * chunk
        t_first = jnp.min(jnp.where(inside, tok, float(n)), axis=1, keepdims=True)[0, 0]
        t_last = jnp.max(jnp.where(inside, tok, -1.0), axis=1, keepdims=True)[0, 0]
        j_lo = t_first.astype(jnp.int32) // LANES
        j_hi = t_last.astype(jnp.int32) // LANES + 1
        want = sub + (lo + 1.0)

        def per_blk(j, accs):
            acc_i, acc_g = accs
            c0 = pl.multiple_of(j * LANES, LANES)
            hit = cs_rows[ei, :, pl.ds(c0, LANES)] == want
            acc_i = acc_i + jnp.where(hit, lane + c0.astype(F32), 0.0)
            acc_g = acc_g + jnp.where(hit, aff_rows[ei, :, pl.ds(c0, LANES)], 0.0)
            return acc_i, acc_g

        z = jnp.zeros((chunk, LANES), F32)
        acc_i, acc_g = lax.fori_loop(j_lo, j_hi, per_blk, (z, z))
        idx_ref[0, pl.ds(s0, chunk), :] = jnp.sum(acc_i, axis=1, keepdims=True).astype(jnp.int32)
        gate_ref[0, pl.ds(s0, chunk), :] = jnp.sum(acc_g, axis=1, keepdims=True)
        return carry

    lax.fori_loop(0, cap // chunk, per_chunk, 0)


def route(logits_t, cap):
    e, n = logits_t.shape
    chunk = min(LANES, cap)
    return pl.pallas_call(
        functools.partial(_route_body, n=n, cap=cap, chunk=chunk),
        grid=(e,),
        in_specs=[pl.BlockSpec((e, n), lambda i: (0, 0))],
        out_specs=[pl.BlockSpec((1, cap, 1), lambda i: (i, 0, 0)), pl.BlockSpec((1, cap, 1), lambda i: (i, 0, 0))],
        out_shape=[jax.ShapeDtypeStruct((e, cap, 1), jnp.int32), jax.ShapeDtypeStruct((e, cap, 1), F32)],
        scratch_shapes=[pltpu.VMEM((e, n), F32), pltpu.VMEM((e, n), F32), pltpu.VMEM((e, n), F32),
                        pltpu.VMEM((e, 1, n), F32), pltpu.VMEM((e, 1, n), F32)],
        compiler_params=_cparams("arbitrary"),
        name="route",
    )(logits_t)


def _row_copy(src_ref, dst_ref, src_row, dst_row, sem):
    return pltpu.make_async_copy(src_ref.at[pl.ds(src_row, 1)], dst_ref.at[pl.ds(dst_row, 1)], sem)


def _wait_rows(src_ref, dst_ref, n, sem):
    pltpu.make_async_copy(src_ref.at[pl.ds(0, n)], dst_ref.at[pl.ds(0, n)], sem).wait()


ROW_DMA_UNROLL = 8
DMA_PRIORITIES = 2


def _gate_up_body(idx_ref, f_ref, wg_ref, wu_ref, o_ref, xg_ref, xb_ref, sems, *, tm, n_f, n_tiles):
    fi = pl.program_id(2)
    tile = pl.program_id(0) * pl.num_programs(1) + pl.program_id(1)
    slot = tile % 2
    share = tm // n_f

    def start_row(tile_id, slot_id, r):
        _row_copy(f_ref, xg_ref.at[slot_id], idx_ref[tile_id * tm + r], r, sems.at[slot_id]).start()

    @pl.when((tile == 0) & (fi == 0))
    def _():
        def issue(r, c):
            start_row(0, 0, r)
            return c
        lax.fori_loop(0, tm, issue, 0, unroll=ROW_DMA_UNROLL)

    @pl.when(fi == 0)
    def _():
        _wait_rows(f_ref, xg_ref.at[slot], tm, sems.at[slot])
        _unpack_rows(xg_ref.at[slot], xb_ref)

    nxt = (tile + 1) % n_tiles
    for r in range(share):
        start_row(nxt, 1 - slot, fi * share + r)

    x = xb_ref[...]
    g = jnp.dot(x, wg_ref[0].astype(BF16), preferred_element_type=F32)
    u = jnp.dot(x, wu_ref[0].astype(BF16), preferred_element_type=F32)
    o_ref[0] = (g * (1.0 / (1.0 + jnp.exp(-g))) * u).astype(o_ref.dtype)

    @pl.when((tile == n_tiles - 1) & (fi == n_f - 1))
    def _():
        _wait_rows(f_ref, xg_ref.at[1 - slot], tm, sems.at[1 - slot])


def expert_gate_up(f_packed, idx, e, w_gate, w_up, tm=1024, tf=256):
    c = idx.shape[0] // e
    dh = f_packed.shape[1]
    d = 2 * dh
    f = w_gate.shape[2]
    tm, tf = min(tm, c), min(tf, f)
    n_f = f // tf
    assert c % tm == 0 and tm % n_f == 0
    return pl.pallas_call(
        functools.partial(_gate_up_body, tm=tm, n_f=n_f, n_tiles=e * (c // tm)),
        grid_spec=pltpu.PrefetchScalarGridSpec(
            num_scalar_prefetch=1,
            grid=(e, c // tm, n_f),
            in_specs=[
                pl.BlockSpec(memory_space=pl.ANY),
                pl.BlockSpec((1, d, tf), lambda ei, mi, fi, idx_ref: (ei, 0, fi)),
                pl.BlockSpec((1, d, tf), lambda ei, mi, fi, idx_ref: (ei, 0, fi)),
            ],
            out_specs=pl.BlockSpec((1, tm, tf), lambda ei, mi, fi, idx_ref: (ei, mi, fi)),
            scratch_shapes=[pltpu.VMEM((2, tm, dh), jnp.uint32), pltpu.VMEM((tm, d), BF16),
                            pltpu.SemaphoreType.DMA((2,))],
        ),
        out_shape=jax.ShapeDtypeStruct((e, c, f), BF16),
        compiler_params=_cparams("arbitrary", "arbitrary", "arbitrary"),
        name="expert_gate_up",
    )(idx, f_packed, w_gate, w_up)


def _down_body(h_ref, w_ref, gate_ref, o_ref):
    y = jnp.dot(h_ref[0], w_ref[0].astype(BF16), preferred_element_type=F32) * gate_ref[0]
    o_ref[0] = y.astype(o_ref.dtype)


def expert_down(hid, w_down, gates, tm=1024, tn=512):
    e, c, f = hid.shape
    d = w_down.shape[2]
    tm, tn = min(tm, c), min(tn, d)
    return pl.pallas_call(
        _down_body,
        grid=(e, c // tm, d // tn),
        in_specs=[
            pl.BlockSpec((1, tm, f), lambda ei, mi, ni: (ei, mi, 0)),
            pl.BlockSpec((1, f, tn), lambda ei, mi, ni: (ei, 0, ni)),
            pl.BlockSpec((1, tm, 1), lambda ei, mi, ni: (ei, mi, 0)),
        ],
        out_specs=pl.BlockSpec((1, tm, tn), lambda ei, mi, ni: (ei, mi, ni)),
        out_shape=jax.ShapeDtypeStruct((e, c, d), BF16),
        compiler_params=_cparams("arbitrary", "arbitrary", "arbitrary"),
        name="expert_down",
    )(hid, w_down, gates)


COMBINE_SLOTS = 3


def _combine_body(idx_ref, ye_ref, x_ref, o_ref, buf_ref, sem_in, sem_out, *, rows, tpe, n_tiles):
    del x_ref
    i = pl.program_id(0)
    slot = i % COMBINE_SLOTS
    first = (i % tpe) == 0
    last = (i % tpe) == tpe - 1

    def fetch(tile, s):
        def some(g, c):
            for u in range(ROW_DMA_UNROLL):
                r = g * ROW_DMA_UNROLL + u
                _row_copy(o_ref, buf_ref.at[s], idx_ref[tile * rows + r], r, sem_in.at[s]).start(
                    priority=u % DMA_PRIORITIES)
            return c
        lax.fori_loop(0, rows // ROW_DMA_UNROLL, some, 0)

    def put(tile, s):
        def some(g, c):
            for u in range(ROW_DMA_UNROLL):
                r = g * ROW_DMA_UNROLL + u
                _row_copy(buf_ref.at[s], o_ref, r, idx_ref[tile * rows + r], sem_out.at[s]).start(
                    priority=u % DMA_PRIORITIES)
            return c
        lax.fori_loop(0, rows // ROW_DMA_UNROLL, some, 0)

    def wait_put(s):
        _wait_rows(buf_ref.at[s], o_ref, rows, sem_out.at[s])

    @pl.when((i >= 2) & (((i - 1) % tpe) != 0))
    def _():
        wait_put((i - 2) % COMBINE_SLOTS)

    @pl.when(first)
    def _():
        @pl.when(i >= 1)
        def _():
            wait_put((i - 1) % COMBINE_SLOTS)
        fetch(i, slot)

    @pl.when(jnp.logical_not(last))
    def _():
        fetch(i + 1, (i + 1) % COMBINE_SLOTS)

    _wait_rows(o_ref, buf_ref.at[slot], rows, sem_in.at[slot])
    buf_ref[slot] = buf_ref[slot] + ye_ref[...].astype(F32)
    put(i, slot)

    @pl.when(i == n_tiles - 1)
    def _():
        if n_tiles >= 2 and (n_tiles - 1) % tpe != 0:
            wait_put((n_tiles - 2) % COMBINE_SLOTS)
        wait_put((n_tiles - 1) % COMBINE_SLOTS)


def combine(x, ye, idx, rows, rows_per_expert):
    r, d = ye.shape
    assert rows_per_expert % rows == 0
    return pl.pallas_call(
        functools.partial(_combine_body, rows=rows, tpe=rows_per_expert // rows, n_tiles=r // rows),
        grid_spec=pltpu.PrefetchScalarGridSpec(
            num_scalar_prefetch=1,
            grid=(r // rows,),
            in_specs=[pl.BlockSpec((rows, d), lambda i, idx_ref: (i, 0)), pl.BlockSpec(memory_space=pl.ANY)],
            out_specs=pl.BlockSpec(memory_space=pl.ANY),
            scratch_shapes=[pltpu.VMEM((COMBINE_SLOTS, rows, d), F32), pltpu.SemaphoreType.DMA((COMBINE_SLOTS,)),
                            pltpu.SemaphoreType.DMA((COMBINE_SLOTS,))],
        ),
        out_shape=jax.ShapeDtypeStruct(x.shape, x.dtype),
        input_output_aliases={2: 0},
        compiler_params=_cparams("arbitrary"),
        name="combine",
    )(idx, ye, x)


def _rel_bucket(rel):
    nb = REL_BUCKETS // 2
    max_exact = nb // 2
    ret = jnp.where(rel > 0, nb, 0)
    n = jnp.abs(rel)
    nf = jnp.maximum(n, 1).astype(F32)
    large = max_exact + (jnp.log(nf / max_exact) / math.log(REL_MAX_DIST / max_exact)
                         * (nb - max_exact)).astype(jnp.int32)
    large = jnp.minimum(large, nb - 1)
    return ret + jnp.where(n < max_exact, n, large)


def _bucket_tiles(t):
    r = np.arange(t)[:, None]
    c = np.arange(t)[None, :]
    rel = np.stack([dt * t + (c - r) for dt in (-2, -1, 0, 1, 2)]).astype(np.int32)
    return _rel_bucket(jnp.asarray(rel)).astype(jnp.int32)


def _axial_rope_tables(s):
    rows = s // GRID_W
    row = jnp.repeat(jnp.arange(rows), GRID_W).astype(F32)
    col = jnp.tile(jnp.arange(GRID_W), rows).astype(F32)
    half = HEAD_DIM // 2
    inv = 1.0 / (ROPE_THETA ** (jnp.arange(0, half, 2, dtype=F32) / half))
    ang_r = row[:, None] * inv
    ang_c = col[:, None] * inv
    ang = jnp.concatenate([ang_r, ang_r, ang_c, ang_c], axis=-1)
    sign = np.where((np.arange(HEAD_DIM) % half) < half // 2, -1.0, 1.0).astype(np.float32)
    return jnp.cos(ang), jnp.sin(ang) * sign


def _layer(x, mem, rel_table, g_mix, w_in, g_qa, g_ka, lam_vecs, g_subln, g_qb, g_kb, w_o, g_cross,
           g_mem, wq_x, wk_x, wv_x, g_qx, g_kx, wo_x, g_ffn, w_router, w_gate, w_up, w_down, lam_init):
    b, s, dm = x.shape
    n = b * s
    n_mem = mem.shape[1]
    x2d = x.reshape(n, dm)
    row = lambda v: v.reshape(1, -1)

    h = rmsnorm_cast(x2d, g_mix)
    proj = matmul([h], w_in, out_dtype=BF16).reshape(b, s, -1)
    t_a = min(256, s)
    assert s % t_a == 0 and t_a >= REL_MAX_DIST, "bias tiles need saturated buckets two tiles away"
    bounded_a = _bounded_flag(_score_bound(g_qa, g_ka, jnp.max(jnp.abs(rel_table))))
    oa = diff_attention(proj, bounded_a, rel_table, _bucket_tiles(t_a), row(g_qa), row(g_ka), row(g_subln),
                        lam_vecs, lam_init, t_a)
    cos, sin_signed = _axial_rope_tables(s)
    ob = gqa_attention(proj, _bounded_flag(_score_bound(g_qb, g_kb)), cos, sin_signed, row(g_qb), row(g_kb))
    x1 = matmul([oa.reshape(n, -1), ob.reshape(n, -1)], w_o, res=x2d, tm=512)

    m = rmsnorm_cast(mem.reshape(b * n_mem, dm), g_mem)
    kx = matmul([m], wk_x).reshape(b, n_mem, -1)
    vx = matmul([m], wv_x).reshape(b, n_mem, -1)
    x2, f_packed, logits = cross_ffn_block(x1, kx, vx, g_cross, wq_x.astype(BF16), row(g_qx), row(g_kx),
                                           wo_x.astype(BF16), g_ffn, w_router)

    e = w_router.shape[1]
    cap = EC_FACTOR * n // e
    idx, gates = route(logits.T, cap)
    idx_flat = idx.reshape(e * cap)
    hid = expert_gate_up(f_packed, idx_flat, e, w_gate, w_up)
    ye = expert_down(hid, w_down, gates)
    out = combine(x2, ye.reshape(e * cap, dm), idx_flat, rows=min(256, cap), rows_per_expert=cap)
    return out.reshape(b, s, dm)


def kernel(x_prompt, x_sample, mem_prompt, mem_sample, rel_table, g_mix, w_in, g_qa, g_ka, lam_q1, lam_k1, lam_q2, lam_k2, g_subln, g_qb, g_kb, w_o, g_cross, g_mem, wq_x, wk_x, wv_x, g_qx, g_kx, wo_x, g_ffn, w_router, w_gate, w_up, w_down):
    outs = []
    for x, mem in ((x_prompt, mem_prompt), (x_sample, mem_sample)):
        for l in range(g_mix.shape[0]):
            lam_init = 0.8 - 0.6 * math.exp(-0.3 * l)
            lam_vecs = jnp.stack([lam_q1[l], lam_k1[l], lam_q2[l], lam_k2[l]])
            x = _layer(x, mem, rel_table, g_mix[l], w_in[l], g_qa[l], g_ka[l], lam_vecs, g_subln[l], g_qb[l],
                       g_kb[l], w_o[l], g_cross[l], g_mem[l], wq_x[l], wk_x[l], wv_x[l], g_qx[l], g_kx[l],
                       wo_x[l], g_ffn[l], w_router[l], w_gate[l], w_up[l], w_down[l], lam_init)
        outs.append(x)
    return tuple(outs)
```
